```python
import math
import jax, jax.numpy as jnp
from jax import lax
import numpy as np

D_MODEL = 2048
BATCH = 8
SEQ = 4096
DEPTH = 1

SSD_HEAD_DIM = 64
SSD_D_INNER = D_MODEL
SSD_HEADS = SSD_D_INNER // SSD_HEAD_DIM
SSD_GROUPS = 8
SSD_HEADS_PER_GROUP = SSD_HEADS // SSD_GROUPS
SSD_STATE = 128
SSD_CONV = 5
SSD_CHUNK = 128
SSD_XBC_WIDTH = SSD_D_INNER + 2 * SSD_GROUPS * SSD_STATE
POOL_WIDTH = D_MODEL // 2
POOL_WINDOWS = (2, 4, 8, 16)
POOL_GROUPS = len(POOL_WINDOWS)
POOL_GROUP_DIM = POOL_WIDTH // POOL_GROUPS
N_BRANCHES = 2
IN_PROJ_WIDTH = SSD_D_INNER + SSD_XBC_WIDTH + 2 * SSD_HEADS + POOL_WIDTH + N_BRANCHES * D_MODEL
PEER_HEADS = 8
PEER_N_KEYS = 128
PEER_N_EXPERTS = PEER_N_KEYS * PEER_N_KEYS
PEER_QUERY_DIM = 256
PEER_HALF_DIM = PEER_QUERY_DIM // 2
PEER_TOPK = 16
PEER_TOKEN_BLOCK = 128
EPS = 1e-6

kernel_name = 'hybrid_ssd_pool_peer_encoder_block'


def rmsnorm(x, w):
    xf = x.astype(jnp.float32)
    y = xf * lax.rsqrt(jnp.mean(xf * xf, axis=-1, keepdims=True) + EPS)
    return (y * w.astype(jnp.float32)).astype(x.dtype)


def depthwise_conv_centred(x, w, b):
    pad = (w.shape[0] - 1) // 2
    y = lax.conv_general_dilated(x, w.astype(jnp.float32), window_strides=(1,), padding=[(pad, pad)],
                                 dimension_numbers=('NWC', 'WIO', 'NWC'), feature_group_count=x.shape[-1])
    return y + b.astype(jnp.float32)


def segsum(a):
    l = a.shape[-1]
    cs = jnp.cumsum(a, axis=-1)
    diff = cs[..., :, None] - cs[..., None, :]
    mask = jnp.tril(jnp.ones((l, l), dtype=bool))
    return jnp.where(mask, diff, -jnp.inf)


def ssd_scan(x, dt, A, B, C):
    b, s, g, r, p = x.shape
    n = B.shape[-1]
    c, l = s // SSD_CHUNK, SSD_CHUNK
    X = (x * dt[..., None]).reshape(b, c, l, g, r, p)
    Adt = jnp.transpose((dt * A).reshape(b, c, l, g, r), (0, 3, 4, 1, 2))
    Bc = B.reshape(b, c, l, g, n)
    Cc = C.reshape(b, c, l, g, n)
    A_cs = jnp.cumsum(Adt, axis=-1)
    Lmat = jnp.exp(segsum(Adt))
    CB = jnp.einsum('bclgn,bcsgn->bgcls', Cc, Bc)
    y_diag = jnp.einsum('bgcls,bgrcls,bcsgrp->bclgrp', CB, Lmat, X)
    decay_states = jnp.exp(A_cs[..., -1:] - A_cs)
    states = jnp.einsum('bclgn,bgrcl,bclgrp->cbgrpn', Bc, decay_states, X)
    chunk_decay = jnp.moveaxis(jnp.exp(A_cs[..., -1]), -1, 0)

    def step(h, inp):
        st, dec = inp
        return h * dec[..., None, None] + st, h

    h0 = jnp.zeros((b, g, r, p, n), jnp.float32)
    _, prev = lax.scan(step, h0, (states, chunk_decay))
    y_off = jnp.einsum('bclgn,cbgrpn,bgrcl->bclgrp', Cc, prev, jnp.exp(A_cs))
    return (y_diag + y_off).reshape(b, s, g, r, p)


def multi_scale_pool(xp):
    b, s, w_all = xp.shape
    P = jnp.concatenate([jnp.zeros((b, 1, w_all), jnp.float32), jnp.cumsum(xp, axis=1)], axis=1)
    t = jnp.arange(s)
    outs = []
    for gi, w in enumerate(POOL_WINDOWS):
        lo = jnp.clip(t - w // 2, 0, s)
        hi = jnp.clip(t + w // 2, 0, s)
        sl = slice(gi * POOL_GROUP_DIM, (gi + 1) * POOL_GROUP_DIM)
        Pg = P[..., sl]
        mean = (Pg[:, hi] - Pg[:, lo]) / (hi - lo).astype(jnp.float32)[None, :, None]
        outs.append(mean - xp[..., sl])
    return jnp.concatenate(outs, axis=-1)


def hybrid_mixer(xn, w_in, conv_w, conv_b, dt_bias, a_log, d_skip, ssd_norm_w, w_ssd_branch,
                 w_pool_group, pool_scale, w_pool_branch, w_out):
    b, s, _ = xn.shape
    G, R, P, N, H = SSD_GROUPS, SSD_HEADS_PER_GROUP, SSD_HEAD_DIM, SSD_STATE, SSD_HEADS
    proj = (xn @ w_in).astype(jnp.float32)
    o1 = SSD_D_INNER
    o2 = o1 + SSD_XBC_WIDTH
    o3 = o2 + 2 * H
    o4 = o3 + POOL_WIDTH
    z, xbc, dt_raw, xp, gate_raw = proj[..., :o1], proj[..., o1:o2], proj[..., o2:o3], proj[..., o3:o4], proj[..., o4:]

    xbc = jax.nn.silu(depthwise_conv_centred(xbc, conv_w, conv_b))
    xs = xbc[..., :SSD_D_INNER].reshape(b, s, G, R, P)
    Bm = xbc[..., SSD_D_INNER:SSD_D_INNER + G * N].reshape(b, s, G, N)
    Cm = xbc[..., SSD_D_INNER + G * N:].reshape(b, s, G, N)
    dtb = dt_bias.astype(jnp.float32)
    dt_f = jax.nn.softplus(dt_raw[..., :H] + dtb[0]).reshape(b, s, G, R)
    dt_b = jax.nn.softplus(dt_raw[..., H:] + dtb[1]).reshape(b, s, G, R)
    A = -jnp.exp(a_log.astype(jnp.float32))
    flip = lambda t: jnp.flip(t, axis=1)
    y_f = ssd_scan(xs, dt_f, A[0].reshape(G, R), Bm, Cm)
    y_b = flip(ssd_scan(flip(xs), flip(dt_b), A[1].reshape(G, R), flip(Bm), flip(Cm)))
    y = y_f + y_b + d_skip.astype(jnp.float32).reshape(G, R)[..., None] * xs
    y = y.reshape(b, s, SSD_D_INNER) * jax.nn.silu(z)
    yg = y.reshape(b, s, G, -1)
    yg = yg * lax.rsqrt(jnp.mean(yg * yg, axis=-1, keepdims=True) + EPS)
    y = yg.reshape(b, s, SSD_D_INNER) * ssd_norm_w.astype(jnp.float32)
    y_ssd = y @ w_ssd_branch.astype(jnp.float32)

    pooled = multi_scale_pool(xp).reshape(b, s, POOL_GROUPS, POOL_GROUP_DIM)
    pooled = jnp.einsum('bsgc,gcd->bsgd', pooled, w_pool_group.astype(jnp.float32)).reshape(b, s, POOL_WIDTH)
    y_pool = (pooled * pool_scale.astype(jnp.float32)) @ w_pool_branch.astype(jnp.float32)

    gates = jax.nn.sigmoid(gate_raw)
    merged = gates[..., :D_MODEL] * y_ssd + gates[..., D_MODEL:] * y_pool
    return (merged @ w_out.astype(jnp.float32)).astype(xn.dtype)


def peer(xn, w_query, sub_keys, expert_u, expert_v):
    b, s, d = xn.shape
    T = b * s
    H, K = PEER_HEADS, PEER_TOPK
    xf = xn.reshape(T, d)
    q = (xf @ w_query).reshape(T, H, 2, PEER_HALF_DIM).astype(jnp.float32)
    scores = jnp.einsum('thkd,hknd->thkn', q, sub_keys.astype(jnp.float32))
    s_half, i_half = lax.top_k(scores, K)
    cand_s = (s_half[:, :, 0, :, None] + s_half[:, :, 1, None, :]).reshape(T, H, K * K)
    cand_i = (i_half[:, :, 0, :, None] * PEER_N_KEYS + i_half[:, :, 1, None, :]).reshape(T, H, K * K)
    top_s, top_pos = lax.top_k(cand_s, K)
    idx = jnp.take_along_axis(cand_i, top_pos, axis=-1)
    gate = jax.nn.softmax(top_s, axis=-1)
    nb = T // PEER_TOKEN_BLOCK
    x_b = xf.reshape(nb, PEER_TOKEN_BLOCK, d)
    idx_b = idx.reshape(nb, PEER_TOKEN_BLOCK, H * K)
    gate_b = gate.reshape(nb, PEER_TOKEN_BLOCK, H * K).astype(xn.dtype)

    def block(args):
        xt, it, gt = args
        u = expert_u[it]
        a = jnp.einsum('tkd,td->tk', u, xt)
        coef = jax.nn.gelu(a, approximate=False) * gt
        v = expert_v[it]
        return jnp.einsum('tk,tkd->td', coef, v)

    out = lax.map(block, (x_b, idx_b, gate_b))
    return out.reshape(b, s, d).astype(xn.dtype)


def setup_inputs(seed: int = 0) -> dict:
    key = jax.random.key(seed)
    ks = jax.random.split(key, 20)
    f32 = jnp.float32
    L = DEPTH

    def nrm(k, shape, scale):
        return jax.random.normal(k, shape, f32) * scale

    x = nrm(ks[0], (BATCH, SEQ, D_MODEL), 1.0)
    mixer_norm_w = 1.0 + nrm(ks[1], (L, D_MODEL), 0.01)
    w_in = nrm(ks[2], (L, D_MODEL, IN_PROJ_WIDTH), D_MODEL ** -0.5)
    conv_w = nrm(ks[3], (L, SSD_CONV, 1, SSD_XBC_WIDTH), SSD_CONV ** -0.5)
    conv_b = nrm(ks[4], (L, SSD_XBC_WIDTH), 0.01)
    dt0 = jnp.exp(jax.random.uniform(ks[5], (L, 2, SSD_HEADS), f32, math.log(1e-3), math.log(1e-1)))
    dt_bias = dt0 + jnp.log(-jnp.expm1(-dt0))
    a_log = jnp.log(jax.random.uniform(ks[6], (L, 2, SSD_HEADS), f32, 1.0, 16.0))
    d_skip = 1.0 + nrm(ks[7], (L, SSD_HEADS), 0.01)
    ssd_norm_w = 1.0 + nrm(ks[8], (L, SSD_D_INNER), 0.01)
    w_ssd_branch = nrm(ks[9], (L, SSD_D_INNER, D_MODEL), SSD_D_INNER ** -0.5)
    w_pool_group = nrm(ks[10], (L, POOL_GROUPS, POOL_GROUP_DIM, POOL_GROUP_DIM), POOL_GROUP_DIM ** -0.5)
    pool_scale = 1.0 + nrm(ks[11], (L, POOL_WIDTH), 0.01)
    w_pool_branch = nrm(ks[12], (L, POOL_WIDTH, D_MODEL), POOL_WIDTH ** -0.5)
    w_out = nrm(ks[13], (L, D_MODEL, D_MODEL), D_MODEL ** -0.5)
    ffn_norm_w = 1.0 + nrm(ks[14], (L, D_MODEL), 0.01)
    w_query = nrm(ks[15], (L, D_MODEL, PEER_HEADS * PEER_QUERY_DIM), D_MODEL ** -0.5)
    sub_keys = nrm(ks[16], (L, PEER_HEADS, 2, PEER_N_KEYS, PEER_HALF_DIM), PEER_HALF_DIM ** -0.5)
    expert_u = nrm(ks[17], (L, PEER_N_EXPERTS, D_MODEL), D_MODEL ** -0.5)
    expert_v = nrm(ks[18], (L, PEER_N_EXPERTS, D_MODEL), 0.25)
    final_norm_w = 1.0 + nrm(ks[19], (D_MODEL,), 0.01)
    return {'x': x, 'mixer_norm_w': mixer_norm_w, 'w_in': w_in, 'conv_w': conv_w, 'conv_b': conv_b,
            'dt_bias': dt_bias, 'a_log': a_log, 'd_skip': d_skip, 'ssd_norm_w': ssd_norm_w,
            'w_ssd_branch': w_ssd_branch, 'w_pool_group': w_pool_group, 'pool_scale': pool_scale,
            'w_pool_branch': w_pool_branch, 'w_out': w_out, 'ffn_norm_w': ffn_norm_w, 'w_query': w_query,
            'sub_keys': sub_keys, 'expert_u': expert_u, 'expert_v': expert_v, 'final_norm_w': final_norm_w}


def reference(x, mixer_norm_w, w_in, conv_w, conv_b, dt_bias, a_log, d_skip, ssd_norm_w, w_ssd_branch,
              w_pool_group, pool_scale, w_pool_branch, w_out, ffn_norm_w, w_query, sub_keys, expert_u,
              expert_v, final_norm_w):
    h = x
    for i in range(DEPTH):
        h = h + hybrid_mixer(rmsnorm(h, mixer_norm_w[i]), w_in[i], conv_w[i], conv_b[i], dt_bias[i], a_log[i],
                             d_skip[i], ssd_norm_w[i], w_ssd_branch[i], w_pool_group[i], pool_scale[i],
                             w_pool_branch[i], w_out[i])
        h = h + peer(rmsnorm(h, ffn_norm_w[i]), w_query[i], sub_keys[i], expert_u[i], expert_v[i])
    return rmsnorm(h, final_norm_w)
```

```python
import functools
import math

import numpy as np
import jax
import jax.numpy as jnp
from jax import lax
from jax.experimental import pallas as pl
from jax.experimental.pallas import tpu as pltpu

F32 = jnp.float32
BF = jnp.bfloat16

LANES = 128
V7X_VMEM_BYTES = 64 * 1024 * 1024
VMEM_LIMIT = 56 * 1024 * 1024

D_MODEL = 2048
SSD_HEAD_DIM = 64
SSD_D_INNER = D_MODEL
SSD_HEADS = SSD_D_INNER // SSD_HEAD_DIM
SSD_GROUPS = 8
SSD_HPG = SSD_HEADS // SSD_GROUPS
SSD_STATE = 128
SSD_CONV = 5
CHUNK = 128
SSD_XBC = SSD_D_INNER + 2 * SSD_GROUPS * SSD_STATE
GROUP_W = SSD_HPG * SSD_HEAD_DIM
POOL_WIDTH = D_MODEL // 2
POOL_WINDOWS = (2, 4, 8, 16)
POOL_GDIM = POOL_WIDTH // len(POOL_WINDOWS)
POOL_HALO = 64
CONV_HALO = 16
PEER_HEADS = 8
PEER_KEYS = 128
PEER_EXPERTS = PEER_KEYS * PEER_KEYS
PEER_HALF = 128
PEER_TOPK = 16
EPS = 1e-6
NEG_BIG = -1e30

MAIN_W = SSD_D_INNER + SSD_XBC + 2 * D_MODEL + POOL_WIDTH
COL_Z = 0
COL_XBC = SSD_D_INNER
COL_GATE = COL_XBC + SSD_XBC
COL_POOL = COL_GATE + 2 * D_MODEL


def _sigmoid(v):
    return 1.0 / (1.0 + jnp.exp(-v))


def _rms(v, w):
    return v * lax.rsqrt(jnp.mean(v * v, axis=-1, keepdims=True) + EPS) * w


def _cparams(sem):
    return pltpu.CompilerParams(dimension_semantics=sem, vmem_limit_bytes=VMEM_LIMIT)


def _resident(shape):
    nd = len(shape)
    return pl.BlockSpec(shape, lambda *_: (0,) * nd, pipeline_mode=pl.Buffered(1))


def _inproj_kernel(x_ref, nw_ref, w_ref, wdt_ref, o_ref, dt_ref, xn_ref):
    tm = x_ref.shape[0]

    @pl.when(pl.program_id(1) == 0)
    def _():
        def body(r, carry):
            rows = pl.ds(pl.multiple_of(r * CHUNK, CHUNK), CHUNK)
            xn = _rms(x_ref[rows, :], nw_ref[...]).astype(BF)
            xn_ref[rows, :] = xn
            dt_ref[rows, :] = jnp.dot(xn, wdt_ref[...], preferred_element_type=F32)
            return carry

        lax.fori_loop(0, tm // CHUNK, body, 0)

    o_ref[...] = jnp.dot(xn_ref[...], w_ref[...], preferred_element_type=F32).astype(BF)


def _in_proj(x2, norm_w, w_main, w_dt, tm=512, tn=1024):
    T, D = x2.shape
    N = w_main.shape[1]
    return pl.pallas_call(
        _inproj_kernel,
        out_shape=(jax.ShapeDtypeStruct((T, N), BF), jax.ShapeDtypeStruct((T, LANES), F32)),
        grid=(T // tm, N // tn),
        in_specs=[
            pl.BlockSpec((tm, D), lambda i, j: (i, 0)),
            pl.BlockSpec((1, D), lambda i, j: (0, 0)),
            pl.BlockSpec((D, tn), lambda i, j: (0, j)),
            pl.BlockSpec((D, LANES), lambda i, j: (0, 0)),
        ],
        out_specs=(
            pl.BlockSpec((tm, tn), lambda i, j: (i, j)),
            pl.BlockSpec((tm, LANES), lambda i, j: (i, 0)),
        ),
        scratch_shapes=[pltpu.VMEM((tm, D), BF)],
        compiler_params=_cparams(("parallel", "arbitrary")),
        name="in_proj",
    )(x2, norm_w, w_main, w_dt)


def _split3(a):
    hi = a.astype(BF)
    r1 = a - hi.astype(F32)
    mid = r1.astype(BF)
    lo = (r1 - mid.astype(F32)).astype(BF)
    return hi, mid, lo


def _dtprep_kernel(raw_ref, bias_ref, alog_ref, tri_ref, dtcs_ref, cst_ref):
    v = raw_ref[...] + bias_ref[...]
    dt = jnp.maximum(v, 0.0) + jnp.log(1.0 + jnp.exp(-jnp.abs(v)))
    a = dt * (-jnp.exp(alog_ref[...]))
    tri = tri_ref[...]
    cs2 = None
    for part in _split3(a):
        t = jnp.dot(tri, part, preferred_element_type=F32)
        cs2 = t if cs2 is None else cs2 + t
    lane = lax.broadcasted_iota(jnp.int32, (CHUNK, LANES), 1)
    backward = (lane % (2 * SSD_HPG)) >= SSD_HPG
    cs = jnp.where(backward, cs2[CHUNK:, :], cs2[:CHUNK, :])
    cst = cs.T
    nh = 2 * SSD_HPG
    for g in range(SSD_GROUPS):
        d = dt if g == 0 else pltpu.roll(dt, (LANES - nh * g) % LANES, axis=1)
        c = pltpu.roll(cs, (LANES + nh - nh * g) % LANES, axis=1)
        dtcs_ref[0, g] = jnp.where(lane < nh, d, jnp.where(lane < 2 * nh, c, 0.0))
        cst_ref[0, g] = cst[nh * g:nh * (g + 1), :]


def _dt_prep(dt_raw3, bias_row, alog_row, tri):
    B, S, _ = dt_raw3.shape
    nh = 2 * SSD_HPG
    return pl.pallas_call(
        _dtprep_kernel,
        out_shape=(jax.ShapeDtypeStruct((B, SSD_GROUPS, S, LANES), F32),
                   jax.ShapeDtypeStruct((B, SSD_GROUPS, nh, S), F32)),
        grid=(B, S // CHUNK),
        in_specs=[
            pl.BlockSpec((None, CHUNK, LANES), lambda b, c: (b, c, 0)),
            pl.BlockSpec((1, LANES), lambda b, c: (0, 0)),
            pl.BlockSpec((1, LANES), lambda b, c: (0, 0)),
            pl.BlockSpec((2 * CHUNK, CHUNK), lambda b, c: (0, 0)),
        ],
        out_specs=(
            pl.BlockSpec((1, SSD_GROUPS, CHUNK, LANES), lambda b, c: (b, 0, c, 0)),
            pl.BlockSpec((1, SSD_GROUPS, nh, CHUNK), lambda b, c: (b, 0, 0, c)),
        ),
        compiler_params=_cparams(("parallel", "parallel")),
        name="dt_prep",
    )(dt_raw3, bias_row, alog_row, tri)


def _ssd_kernel(xs_ref, b_ref, c_ref, z_ref, dtcs_ref, cst_ref, cw_ref, cb_ref, dsk_ref, nw_ref,
                y_ref, xbc_s, bt_s, yf_s, yb_s, h_s):
    S = xs_ref.shape[1]
    nc = S // CHUNK
    H = CONV_HALO
    N = SSD_STATE

    def conv_part(src_ref, c, w, bias):
        r0 = pl.multiple_of(c * CHUNK, CHUNK)
        cur = src_ref[0, pl.ds(r0, CHUNK), :].astype(F32)
        pidx = pl.multiple_of(jnp.maximum(r0 - H, 0), H)
        nidx = pl.multiple_of(jnp.minimum(r0 + CHUNK, S - H), H)
        prev = src_ref[0, pl.ds(pidx, H), :].astype(F32) * (c > 0).astype(F32)
        nxt = src_ref[0, pl.ds(nidx, H), :].astype(F32) * (c < nc - 1).astype(F32)
        cat = jnp.concatenate([prev, cur, nxt], axis=0)
        rows = CHUNK + 2 * H
        acc = bias + w[2:3, :] * cur
        for k in (0, 1, 3, 4):
            acc = acc + w[k:k + 1, :] * pltpu.roll(cat, (2 - k) % rows, axis=0)[H:H + CHUNK, :]
        return acc * _sigmoid(acc)

    def conv_body(c, carry):
        rows = pl.ds(pl.multiple_of(c * CHUNK, CHUNK), CHUNK)
        w = cw_ref[0]
        bias = cb_ref[0]
        xs = conv_part(xs_ref, c, w[:, :GROUP_W], bias[:, :GROUP_W])
        bm = conv_part(b_ref, c, w[:, GROUP_W:GROUP_W + N], bias[:, GROUP_W:GROUP_W + N])
        cm = conv_part(c_ref, c, w[:, GROUP_W + N:], bias[:, GROUP_W + N:])
        xbc_s[rows, 0:GROUP_W] = xs.astype(BF)
        xbc_s[rows, GROUP_W:GROUP_W + N] = bm.astype(BF)
        xbc_s[rows, GROUP_W + N:] = cm.astype(BF)
        bt_s[:, rows] = bm.T.astype(BF)
        return carry

    lax.fori_loop(0, nc, conv_body, 0)

    h_s[...] = jnp.zeros_like(h_s)
    li = lax.broadcasted_iota(jnp.int32, (CHUNK, CHUNK), 0)
    si = lax.broadcasted_iota(jnp.int32, (CHUNK, CHUNK), 1)
    lo_half = si < SSD_HEAD_DIM

    def expand(cols):
        bc = [jnp.broadcast_to(v, (CHUNK, LANES)) for v in cols]
        return jnp.concatenate([jnp.where(lo_half, bc[0], bc[1]), jnp.where(lo_half, bc[2], bc[3])], axis=1), bc

    def chunk_step(c, d):
        rows = pl.ds(pl.multiple_of(c * CHUNK, CHUNK), CHUNK)
        xs = xbc_s[rows, 0:GROUP_W]
        bm_t = bt_s[:, rows]
        cm = xbc_s[rows, GROUP_W + N:]
        dtcs = dtcs_ref[0, 0, rows, :]
        cst = cst_ref[0, 0, :, rows]
        o = SSD_HPG * d
        dtx, _ = expand([dtcs[:, o + r:o + r + 1] for r in range(SSD_HPG)])
        csx, csb = expand([dtcs[:, 2 * SSD_HPG + o + r:2 * SSD_HPG + o + r + 1] for r in range(SSD_HPG)])
        causal = (li >= si) if d == 0 else (li <= si)
        edge = csx[CHUNK - 1:CHUNK, :] if d == 0 else csx[0:1, :]
        xdt = xs.astype(F32) * dtx
        xdec = (xdt * jnp.exp(edge - csx)).astype(BF)
        hprev = h_s[d]
        yoff = jnp.dot(cm, hprev.astype(BF), preferred_element_type=F32) * jnp.exp(csx)
        h_s[d] = hprev * jnp.exp(edge) + jnp.dot(bm_t, xdec, preferred_element_type=F32)
        cb = jnp.dot(cm, bm_t, preferred_element_type=F32)
        ys = []
        for p in range(2):
            xp = xdt[:, LANES * p:LANES * (p + 1)]
            acc = None
            for q in range(2):
                r = 2 * p + q
                xq = jnp.where(lo_half, xp, 0.0) if q == 0 else jnp.where(lo_half, 0.0, xp)
                diff = csb[r] - cst[o + r:o + r + 1, :]
                m = (cb * jnp.exp(jnp.where(causal, diff, NEG_BIG))).astype(BF)
                t = jnp.dot(m, xq.astype(BF), preferred_element_type=F32)
                acc = t if acc is None else acc + t
            ys.append(acc)
        y = jnp.concatenate(ys, axis=1) + yoff
        if d == 0:
            yf_s[rows, :] = y
        else:
            yb_s[rows, :] = y

    def scan_body(i, carry):
        chunk_step(i, 0)
        chunk_step(nc - 1 - i, 1)
        return carry

    lax.fori_loop(0, nc, scan_body, 0)

    def out_body(c, carry):
        rows = pl.ds(pl.multiple_of(c * CHUNK, CHUNK), CHUNK)
        xs = xbc_s[rows, 0:GROUP_W].astype(F32)
        y = yf_s[rows, :] + yb_s[rows, :] + dsk_ref[0] * xs
        z = z_ref[0, rows, :].astype(F32)
        y = y * (z * _sigmoid(z))
        y_ref[0, rows, :] = _rms(y, nw_ref[0]).astype(BF)
        return carry

    lax.fori_loop(0, nc, out_body, 0)


def _ssd(proj3, dtcs, cst, conv_w_g, conv_b_g, dskip_g, normw_g):
    B, S, _ = proj3.shape
    G, N, W = SSD_GROUPS, SSD_STATE, GROUP_W
    xs_blk = COL_XBC // W
    b_blk = (COL_XBC + SSD_D_INNER) // N
    c_blk = b_blk + G
    return pl.pallas_call(
        _ssd_kernel,
        out_shape=jax.ShapeDtypeStruct((B, S, SSD_D_INNER), BF),
        grid=(B, G),
        in_specs=[
            pl.BlockSpec((1, S, W), lambda b, g: (b, 0, xs_blk + g)),
            pl.BlockSpec((1, S, N), lambda b, g: (b, 0, b_blk + g)),
            pl.BlockSpec((1, S, N), lambda b, g: (b, 0, c_blk + g)),
            pl.BlockSpec((1, S, W), lambda b, g: (b, 0, g)),
            pl.BlockSpec((1, 1, S, LANES), lambda b, g: (b, g, 0, 0)),
            pl.BlockSpec((1, 1, 2 * SSD_HPG, S), lambda b, g: (b, g, 0, 0)),
            pl.BlockSpec((1, 8, W + 2 * N), lambda b, g: (g, 0, 0)),
            pl.BlockSpec((1, 1, W + 2 * N), lambda b, g: (g, 0, 0)),
            pl.BlockSpec((1, 1, W), lambda b, g: (g, 0, 0)),
            pl.BlockSpec((1, 1, W), lambda b, g: (g, 0, 0)),
        ],
        out_specs=pl.BlockSpec((1, S, W), lambda b, g: (b, 0, g)),
        scratch_shapes=[
            pltpu.VMEM((S, W + 2 * N), BF),
            pltpu.VMEM((N, S), BF),
            pltpu.VMEM((S, W), F32),
            pltpu.VMEM((S, W), F32),
            pltpu.VMEM((2, N, W), F32),
        ],
        compiler_params=_cparams(("parallel", "parallel")),
        name="ssd",
    )(proj3, proj3, proj3, proj3, dtcs, cst, conv_w_g, conv_b_g, dskip_g, normw_g)


def _merge_kernel(seq_len, ys_ref, xp_ref, xprev_ref, xnext_ref, gate1_ref, gate2_ref, x_ref,
                  wssd_ref, band_ref, wg_ref, pscale_ref, wpool_ref, wout_ref, fnw_ref,
                  h_ref, hnt_ref):
    tm = ys_ref.shape[0]
    i = pl.program_id(0)
    pos0 = (i * tm) % seq_len
    yssd = jnp.dot(ys_ref[...], wssd_ref[...], preferred_element_type=F32)

    cur = xp_ref[...]
    zero = jnp.zeros((POOL_HALO, POOL_WIDTH), BF)
    prev = jnp.where(pos0 > 0, xprev_ref[...], zero)
    nxt = jnp.where(pos0 + tm < seq_len, xnext_ref[...], zero)
    cat = jnp.concatenate([prev, cur, nxt], axis=0)
    pos = pos0 + lax.broadcasted_iota(jnp.int32, (tm, 1), 0)
    pooled = []
    for gi, w in enumerate(POOL_WINDOWS):
        cols = slice(gi * POOL_GDIM, (gi + 1) * POOL_GDIM)
        wsum = jnp.dot(band_ref[gi], cat[:, cols], preferred_element_type=F32)
        cnt = jnp.minimum(pos + w // 2, seq_len) - jnp.maximum(pos - w // 2, 0)
        pg = wsum / cnt.astype(F32) - cur[:, cols].astype(F32)
        pg = jnp.dot(pg.astype(BF), wg_ref[gi], preferred_element_type=F32)
        pooled.append((pg * pscale_ref[:, cols]).astype(BF))
    ypool = jnp.dot(jnp.concatenate(pooled, axis=1), wpool_ref[...], preferred_element_type=F32)

    merged = _sigmoid(gate1_ref[...].astype(F32)) * yssd + _sigmoid(gate2_ref[...].astype(F32)) * ypool
    h = x_ref[...] + jnp.dot(merged.astype(BF), wout_ref[...], preferred_element_type=F32)
    h_ref[...] = h
    hnt_ref[...] = _rms(h, fnw_ref[...]).T.astype(BF)


def _merge(y_ssd, proj, x2, w_ssd, band, w_g, pool_scale, w_pool, w_out, ffn_w, seq_len, tm=256):
    T, D = x2.shape
    hb = tm // POOL_HALO
    last_hb = T // POOL_HALO - 1
    pool_blk = COL_POOL // POOL_WIDTH
    g1_blk = COL_GATE // D
    return pl.pallas_call(
        functools.partial(_merge_kernel, seq_len),
        out_shape=(jax.ShapeDtypeStruct((T, D), F32), jax.ShapeDtypeStruct((D, T), BF)),
        grid=(T // tm,),
        in_specs=[
            pl.BlockSpec((tm, D), lambda i: (i, 0)),
            pl.BlockSpec((tm, POOL_WIDTH), lambda i: (i, pool_blk)),
            pl.BlockSpec((POOL_HALO, POOL_WIDTH), lambda i: (jnp.maximum(i * hb - 1, 0), pool_blk)),
            pl.BlockSpec((POOL_HALO, POOL_WIDTH), lambda i: (jnp.minimum((i + 1) * hb, last_hb), pool_blk)),
            pl.BlockSpec((tm, D), lambda i: (i, g1_blk)),
            pl.BlockSpec((tm, D), lambda i: (i, g1_blk + 1)),
            pl.BlockSpec((tm, D), lambda i: (i, 0)),
            _resident(w_ssd.shape),
            _resident(band.shape),
            _resident(w_g.shape),
            _resident(pool_scale.shape),
            _resident(w_pool.shape),
            _resident(w_out.shape),
            _resident(ffn_w.shape),
        ],
        out_specs=(
            pl.BlockSpec((tm, D), lambda i: (i, 0)),
            pl.BlockSpec((D, tm), lambda i: (0, i)),
        ),
        compiler_params=_cparams(("parallel",)),
        name="merge",
    )(y_ssd, proj, proj, proj, proj, proj, x2, w_ssd, band, w_g, pool_scale, w_pool, w_out, ffn_w)


def _cmpx(v, i, j):
    hi = jnp.maximum(v[i], v[j])
    lo = jnp.minimum(v[i], v[j])
    v[i], v[j] = hi, lo


def _bitonic_merge16(v):
    v = list(v)
    for j in (8, 4, 2, 1):
        for i in range(16):
            if i ^ j > i:
                _cmpx(v, i, i ^ j)
    return v


def _bitonic_sort16(v):
    v = list(v)
    for k in (2, 4, 8):
        for j in [s for s in (4, 2, 1) if s < k]:
            for i in range(16):
                l = i ^ j
                if l > i:
                    if i & k == 0:
                        _cmpx(v, i, l)
                    else:
                        _cmpx(v, l, i)
    return _bitonic_merge16(v)


def _top16_union(p, q):
    return _bitonic_merge16([jnp.maximum(p[i], q[15 - i]) for i in range(16)])


def _route_kernel(hnt_ref, wq_ref, keys_ref, r1_ref, vh_ref, n_ref, u_ref, sc_s, ac_s, bc_s, tz_s):
    tm = hnt_ref.shape[1]
    nlb = tm // LANES
    K = PEER_TOPK

    def score_body(hk, carry):
        rows = pl.ds(pl.multiple_of(hk * PEER_HALF, PEER_HALF), PEER_HALF)
        q = jnp.dot(wq_ref[rows, :], hnt_ref[...], preferred_element_type=F32).astype(BF)
        sc_s[hk] = jnp.dot(keys_ref[hk], q, preferred_element_type=F32)
        return carry

    lax.fori_loop(0, 2 * PEER_HEADS, score_body, 0)
    ac_s[...] = jnp.zeros_like(ac_s)
    bc_s[...] = jnp.zeros_like(bc_s)

    sub = lax.broadcasted_iota(jnp.int32, (8, LANES), 0)

    def top_body(idx, carry):
        h = idx // nlb
        lanes = pl.ds(pl.multiple_of((idx % nlb) * LANES, LANES), LANES)
        for half, dst in ((0, ac_s), (1, bc_s)):
            v = _bitonic_sort16([sc_s[2 * h + half, pl.ds(8 * k, 8), lanes] for k in range(PEER_KEYS // 8)])
            for sh in (4, 2, 1):
                v = _bitonic_merge16([jnp.maximum(v[i], pltpu.roll(v[15 - i], sh, axis=0)) for i in range(16)])
            for r in range(K):
                dst[r, :, lanes] = jnp.where(sub == h, v[r], dst[r, :, lanes])
        return carry

    lax.fori_loop(0, PEER_HEADS * nlb, top_body, 0)

    def pair_body(lb, carry):
        lanes = pl.ds(pl.multiple_of(lb * LANES, LANES), LANES)
        a = [ac_s[r, :, lanes] for r in range(K)]
        b = [bc_s[r, :, lanes] for r in range(K)]
        first = [a[0] + b[j] for j in range(K)]
        rest = [a[i] + b[j] for i in range(1, K) for j in range(K // (i + 1))]
        rest = rest + [jnp.full_like(a[0], -jnp.inf)] * (3 * K - len(rest))
        groups = [_bitonic_sort16(rest[K * g:K * (g + 1)]) for g in range(3)]
        top = _top16_union(_top16_union(first, groups[0]), _top16_union(groups[1], groups[2]))
        z = None
        for r in range(K):
            e = jnp.exp(top[r] - top[0])
            z = e if z is None else z + e
        tz_s[0, :, lanes] = top[K - 1]
        tz_s[1, :, lanes] = 1.0 / z
        return carry

    lax.fori_loop(0, nlb, pair_body, 0)

    def table_body(lb, carry):
        lanes = pl.ds(pl.multiple_of(lb * LANES, LANES), LANES)
        for h in range(PEER_HEADS):
            s0 = sc_s[2 * h, :, lanes]
            s1 = sc_s[2 * h + 1, :, lanes]
            tau = tz_s[0, h:h + 1, lanes]
            rank = jnp.ones_like(s1)
            cnt = jnp.zeros_like(s0)
            for r in range(K):
                br = bc_s[r, h:h + 1, lanes]
                rank = rank + jnp.where(br > s1, 1.0, 0.0)
                cnt = cnt + jnp.where(s0 + br >= tau, 1.0, 0.0)
            r1_ref[h, :, lanes] = rank.astype(BF)
            n_ref[h, :, lanes] = cnt
            vh_ref[h, :, lanes] = jnp.exp(s1 - bc_s[0, h:h + 1, lanes]).astype(BF)
            u_ref[h, :, lanes] = jnp.exp(s0 - ac_s[0, h:h + 1, lanes]) * tz_s[1, h:h + 1, lanes]
        return carry

    lax.fori_loop(0, nlb, table_body, 0)


def _route(hnt, wq_t, keys, tm=512):
    D, T = hnt.shape
    H, NK = PEER_HEADS, PEER_KEYS
    tbl = lambda dt: jax.ShapeDtypeStruct((H, NK, T), dt)
    tspec = pl.BlockSpec((H, NK, tm), lambda i: (0, 0, i))
    return pl.pallas_call(
        _route_kernel,
        out_shape=(tbl(BF), tbl(BF), tbl(F32), tbl(F32)),
        grid=(T // tm,),
        in_specs=[
            pl.BlockSpec((D, tm), lambda i: (0, i)),
            _resident(wq_t.shape),
            _resident(keys.shape),
        ],
        out_specs=(tspec, tspec, tspec, tspec),
        scratch_shapes=[
            pltpu.VMEM((2 * H, NK, tm), F32),
            pltpu.VMEM((PEER_TOPK, H, tm), F32),
            pltpu.VMEM((PEER_TOPK, H, tm), F32),
            pltpu.VMEM((2, H, tm), F32),
        ],
        compiler_params=_cparams(("parallel",)),
        name="route",
    )(hnt, wq_t, keys)


def _experts_kernel(hnt_ref, u_ref, vt_ref, r1_ref, vh_ref, n_ref, uu_ref, h_ref, fw_ref,
                    o_ref, acc_s, coef_s):
    eb = u_ref.shape[0]
    tm = hnt_ref.shape[1]
    e = pl.program_id(1)

    @pl.when(e == 0)
    def _():
        acc_s[...] = jnp.zeros_like(acc_s)

    a_t = jnp.dot(u_ref[...], hnt_ref[...], preferred_element_type=F32)
    for ib in range(eb // PEER_KEYS):
        a = a_t[ib * PEER_KEYS:(ib + 1) * PEER_KEYS, :]
        act = (0.5 * a * (1.0 + lax.erf(a * math.sqrt(0.5)))).astype(BF)
        gate = jnp.zeros((PEER_KEYS, tm), BF)
        for h in range(PEER_HEADS):
            nrow = jnp.broadcast_to(n_ref[h, ib:ib + 1, :], (PEER_KEYS, tm)).astype(BF)
            urow = jnp.broadcast_to(uu_ref[h, ib:ib + 1, :], (PEER_KEYS, tm)).astype(BF)
            sel = jnp.where(r1_ref[h] <= nrow, vh_ref[h], jnp.zeros((PEER_KEYS, tm), BF))
            gate = gate + sel * urow
        coef_s[ib * PEER_KEYS:(ib + 1) * PEER_KEYS, :] = act * gate
    acc_s[...] += jnp.dot(vt_ref[...], coef_s[...], preferred_element_type=F32)

    @pl.when(e == pl.num_programs(1) - 1)
    def _():
        o_ref[...] = _rms(h_ref[...] + acc_s[...].T, fw_ref[...])


def _experts(hnt, u_b, v_t, r1, vh, n, uu, h, final_w, tm=512):
    D, T = hnt.shape
    E = u_b.shape[0]
    H, NK = PEER_HEADS, PEER_KEYS
    kb = 8
    eb = kb * NK
    tspec = pl.BlockSpec((H, NK, tm), lambda i, e: (0, 0, i), pipeline_mode=pl.Buffered(1))
    kspec = pl.BlockSpec((H, kb, tm), lambda i, e: (0, e, i))
    return pl.pallas_call(
        _experts_kernel,
        out_shape=jax.ShapeDtypeStruct((T, D), F32),
        grid=(T // tm, E // eb),
        in_specs=[
            pl.BlockSpec((D, tm), lambda i, e: (0, i), pipeline_mode=pl.Buffered(1)),
            pl.BlockSpec((eb, D), lambda i, e: (e, 0)),
            pl.BlockSpec((D, eb), lambda i, e: (0, e)),
            tspec, tspec, kspec, kspec,
            pl.BlockSpec((tm, D), lambda i, e: (i, 0), pipeline_mode=pl.Buffered(1)),
            pl.BlockSpec((1, D), lambda i, e: (0, 0)),
        ],
        out_specs=pl.BlockSpec((tm, D), lambda i, e: (i, 0)),
        scratch_shapes=[pltpu.VMEM((D, tm), F32), pltpu.VMEM((eb, tm), BF)],
        compiler_params=_cparams(("parallel", "arbitrary")),
        name="experts",
    )(hnt, u_b, v_t, r1, vh, n, uu, h, final_w)


def _band_matrices(tm):
    t = np.arange(tm)[:, None]
    j = np.arange(tm + 2 * POOL_HALO)[None, :]
    return np.stack([((j >= t + POOL_HALO - w // 2) & (j < t + POOL_HALO + w // 2)) for w in POOL_WINDOWS]
                    ).astype(np.float32)


def _tri_matrix():
    l = np.arange(CHUNK)[:, None]
    s = np.arange(CHUNK)[None, :]
    return np.concatenate([(s <= l), (s >= l)], axis=0).astype(np.float32)


def _dt_perm():
    cols = []
    for g in range(SSD_GROUPS):
        for d in range(2):
            for r in range(SSD_HPG):
                cols.append(d * SSD_HEADS + g * SSD_HPG + r)
    return np.asarray(cols)


def kernel(x, mixer_norm_w, w_in, conv_w, conv_b, dt_bias, a_log, d_skip, ssd_norm_w, w_ssd_branch, w_pool_group, pool_scale, w_pool_branch, w_out, ffn_norm_w, w_query, sub_keys, expert_u, expert_v, final_norm_w):
    return _forward(x, mixer_norm_w, w_in, conv_w, conv_b, dt_bias, a_log, d_skip, ssd_norm_w, w_ssd_branch,
                    w_pool_group, pool_scale, w_pool_branch, w_out, ffn_norm_w, w_query, sub_keys, expert_u,
                    expert_v, final_norm_w)[0]


def _forward(x, mixer_norm_w, w_in, conv_w, conv_b, dt_bias, a_log, d_skip, ssd_norm_w, w_ssd_branch, w_pool_group, pool_scale, w_pool_branch, w_out, ffn_norm_w, w_query, sub_keys, expert_u, expert_v, final_norm_w):
    B, S, D = x.shape
    T = B * S
    G, N, W = SSD_GROUPS, SSD_STATE, GROUP_W
    assert D == D_MODEL and mixer_norm_w.shape[0] == 1 and S % 512 == 0
    merge_tm = 256

    w_in0 = w_in[0]
    o1 = SSD_D_INNER
    o2 = o1 + SSD_XBC
    o3 = o2 + 2 * SSD_HEADS
    o4 = o3 + POOL_WIDTH
    w_main = jnp.concatenate([w_in0[:, :o2], w_in0[:, o4:], w_in0[:, o3:o4]], axis=1).astype(BF)
    perm = _dt_perm()
    pad = LANES - perm.size
    w_dt = jnp.pad(w_in0[:, o2:o3][:, perm], ((0, 0), (0, pad))).astype(BF)
    bias_row = jnp.pad(dt_bias[0].reshape(-1)[perm], (0, pad)).reshape(1, LANES)
    alog_row = jnp.pad(a_log[0].reshape(-1)[perm], (0, pad)).reshape(1, LANES)

    cw = conv_w[0, :, 0, :]
    cwx = cw[:, :SSD_D_INNER].reshape(SSD_CONV, G, W)
    cwb = cw[:, SSD_D_INNER:SSD_D_INNER + G * N].reshape(SSD_CONV, G, N)
    cwc = cw[:, SSD_D_INNER + G * N:].reshape(SSD_CONV, G, N)
    conv_w_g = jnp.pad(jnp.transpose(jnp.concatenate([cwx, cwb, cwc], axis=2), (1, 0, 2)),
                       ((0, 0), (0, 8 - SSD_CONV), (0, 0)))
    cbv = conv_b[0]
    conv_b_g = jnp.concatenate([cbv[:SSD_D_INNER].reshape(G, 1, W),
                                cbv[SSD_D_INNER:SSD_D_INNER + G * N].reshape(G, 1, N),
                                cbv[SSD_D_INNER + G * N:].reshape(G, 1, N)], axis=2)
    dskip_g = jnp.repeat(d_skip[0], SSD_HEAD_DIM).reshape(G, 1, W)
    normw_g = ssd_norm_w[0].reshape(G, 1, W)

    wq_t = w_query[0].T.astype(BF)
    keys = sub_keys[0].reshape(2 * PEER_HEADS, PEER_KEYS, PEER_HALF).astype(BF)
    u_b = expert_u[0].astype(BF)
    v_t = expert_v[0].T.astype(BF)

    x2 = x.reshape(T, D)
    proj, dt_raw = _in_proj(x2, mixer_norm_w, w_main, w_dt)
    dtcs, cst = _dt_prep(dt_raw.reshape(B, S, LANES), bias_row, alog_row, jnp.asarray(_tri_matrix(), BF))
    y_ssd = _ssd(proj.reshape(B, S, MAIN_W), dtcs, cst, conv_w_g, conv_b_g, dskip_g, normw_g)
    h, hnt = _merge(y_ssd.reshape(T, SSD_D_INNER), proj, x2,
                    w_ssd_branch[0].astype(BF), jnp.asarray(_band_matrices(merge_tm), BF),
                    w_pool_group[0].astype(BF), pool_scale, w_pool_branch[0].astype(BF),
                    w_out[0].astype(BF), ffn_norm_w, S, tm=merge_tm)
    r1, vh, n, uu = _route(hnt, wq_t, keys)
    out = _experts(hnt, u_b, v_t, r1, vh, n, uu, h, final_norm_w.reshape(1, D))
    return out.reshape(B, S, D), h
```

```python
import functools
import math

import numpy as np
import jax
import jax.numpy as jnp
from jax import lax
from jax.experimental import pallas as pl
from jax.experimental.pallas import tpu as pltpu

F32 = jnp.float32
BF = jnp.bfloat16

LANES = 128
V7X_VMEM_BYTES = 64 * 1024 * 1024
VMEM_LIMIT = 56 * 1024 * 1024

D_MODEL = 2048
SSD_HEAD_DIM = 64
SSD_D_INNER = D_MODEL
SSD_HEADS = SSD_D_INNER // SSD_HEAD_DIM
SSD_GROUPS = 8
SSD_HPG = SSD_HEADS // SSD_GROUPS
SSD_STATE = 128
SSD_CONV = 5
CHUNK = 128
SSD_XBC = SSD_D_INNER + 2 * SSD_GROUPS * SSD_STATE
GROUP_W = SSD_HPG * SSD_HEAD_DIM
POOL_WIDTH = D_MODEL // 2
POOL_WINDOWS = (2, 4, 8, 16)
POOL_GDIM = POOL_WIDTH // len(POOL_WINDOWS)
POOL_HALO = 64
CONV_HALO = 64
PEER_HEADS = 8
PEER_KEYS = 128
PEER_EXPERTS = PEER_KEYS * PEER_KEYS
PEER_HALF = 128
PEER_TOPK = 16
EPS = 1e-6
NEG_BIG = -1e30

MAIN_W = SSD_D_INNER + SSD_XBC + 2 * D_MODEL + POOL_WIDTH
COL_Z = 0
COL_XBC = SSD_D_INNER
COL_GATE = COL_XBC + SSD_XBC
COL_POOL = COL_GATE + 2 * D_MODEL


def _sigmoid(v):
    return 1.0 / (1.0 + jnp.exp(-v))


def _rms(v, w):
    return v * lax.rsqrt(jnp.mean(v * v, axis=-1, keepdims=True) + EPS) * w


def _cparams(sem):
    return pltpu.CompilerParams(dimension_semantics=sem, vmem_limit_bytes=VMEM_LIMIT)


def _resident(shape):
    nd = len(shape)
    return pl.BlockSpec(shape, lambda *_: (0,) * nd, pipeline_mode=pl.Buffered(1))


def _inproj_kernel(x_ref, nw_ref, w_ref, wdt_ref, o_ref, dt_ref, xn_ref):
    tm = x_ref.shape[0]

    @pl.when(pl.program_id(1) == 0)
    def _():
        def body(r, carry):
            rows = pl.ds(pl.multiple_of(r * CHUNK, CHUNK), CHUNK)
            xn = _rms(x_ref[rows, :], nw_ref[...]).astype(BF)
            xn_ref[rows, :] = xn
            dt_ref[rows, :] = jnp.dot(xn, wdt_ref[...], preferred_element_type=F32)
            return carry

        lax.fori_loop(0, tm // CHUNK, body, 0)

    o_ref[...] = jnp.dot(xn_ref[...], w_ref[...], preferred_element_type=F32).astype(BF)


def _in_proj(x2, norm_w, w_main, w_dt, tm=512, tn=1024):
    T, D = x2.shape
    N = w_main.shape[1]
    return pl.pallas_call(
        _inproj_kernel,
        out_shape=(jax.ShapeDtypeStruct((T, N), BF), jax.ShapeDtypeStruct((T, LANES), F32)),
        grid=(T // tm, N // tn),
        in_specs=[
            pl.BlockSpec((tm, D), lambda i, j: (i, 0)),
            pl.BlockSpec((1, D), lambda i, j: (0, 0)),
            pl.BlockSpec((D, tn), lambda i, j: (0, j)),
            pl.BlockSpec((D, LANES), lambda i, j: (0, 0)),
        ],
        out_specs=(
            pl.BlockSpec((tm, tn), lambda i, j: (i, j)),
            pl.BlockSpec((tm, LANES), lambda i, j: (i, 0)),
        ),
        scratch_shapes=[pltpu.VMEM((tm, D), BF)],
        compiler_params=_cparams(("parallel", "arbitrary")),
        name="in_proj",
    )(x2, norm_w, w_main, w_dt)


def _split3(a):
    hi = a.astype(BF)
    r1 = a - hi.astype(F32)
    mid = r1.astype(BF)
    lo = (r1 - mid.astype(F32)).astype(BF)
    return hi, mid, lo


def _dtprep_kernel(raw_ref, bias_ref, alog_ref, tri_ref, col_ref, row_ref):
    nh = 2 * SSD_HPG
    lane = lax.broadcasted_iota(jnp.int32, (CHUNK, LANES), 1)
    backward = (lane % nh) >= SSD_HPG
    tri = tri_ref[...]
    for ci in range(raw_ref.shape[0] // CHUNK):
        rows = slice(ci * CHUNK, (ci + 1) * CHUNK)
        v = raw_ref[rows, :] + bias_ref[...]
        dt = jnp.maximum(v, 0.0) + jnp.log(1.0 + jnp.exp(-jnp.abs(v)))
        a = dt * (-jnp.exp(alog_ref[...]))
        cs2 = None
        for part in _split3(a):
            t = jnp.dot(tri, part, preferred_element_type=F32)
            cs2 = t if cs2 is None else cs2 + t
        cs = jnp.where(backward, cs2[CHUNK:, :], cs2[:CHUNK, :])
        edge = jnp.where(backward[0:1, :], cs[0:1, :], cs[CHUNK - 1:CHUNK, :])
        w = dt * jnp.exp(edge - cs)
        cs_t = cs.T
        dt_t = dt.T
        for g in range(SSD_GROUPS):
            wg = w if g == 0 else pltpu.roll(w, (LANES - nh * g) % LANES, axis=1)
            cg = pltpu.roll(cs, (LANES + nh - nh * g) % LANES, axis=1)
            col_ref[0, g, rows, :] = jnp.where(lane < nh, wg, jnp.where(lane < 2 * nh, cg, 0.0))
            row_ref[0, g, 0:nh, rows] = cs_t[nh * g:nh * (g + 1), :]
            row_ref[0, g, nh:2 * nh, rows] = dt_t[nh * g:nh * (g + 1), :]


def _dt_prep(dt_raw3, bias_row, alog_row, tri, rows=4 * CHUNK):
    B, S, _ = dt_raw3.shape
    nh = 2 * SSD_HPG
    return pl.pallas_call(
        _dtprep_kernel,
        out_shape=(jax.ShapeDtypeStruct((B, SSD_GROUPS, S, LANES), F32),
                   jax.ShapeDtypeStruct((B, SSD_GROUPS, 2 * nh, S), F32)),
        grid=(B, S // rows),
        in_specs=[
            pl.BlockSpec((None, rows, LANES), lambda b, c: (b, c, 0)),
            pl.BlockSpec((1, LANES), lambda b, c: (0, 0)),
            pl.BlockSpec((1, LANES), lambda b, c: (0, 0)),
            pl.BlockSpec((2 * CHUNK, CHUNK), lambda b, c: (0, 0)),
        ],
        out_specs=(
            pl.BlockSpec((1, SSD_GROUPS, rows, LANES), lambda b, c: (b, 0, c, 0)),
            pl.BlockSpec((1, SSD_GROUPS, 2 * nh, rows), lambda b, c: (b, 0, 0, c)),
        ),
        compiler_params=_cparams(("parallel", "parallel")),
        name="dt_prep",
    )(dt_raw3, bias_row, alog_row, tri)


def _ssd_kernel(xs_ref, b_ref, c_ref, z_ref, col_ref, row_ref, cw_ref, cb_ref, shift_ref, dsk_ref, nw_ref,
                y_ref, xlo_s, xhi_s, cm_s, bt_s, xdf_s, xdb_s, yf_s, yb_s, h_s):
    S = xs_ref.shape[1]
    nc = S // CHUNK
    H = CONV_HALO
    N = SSD_STATE
    hpg = SSD_HPG

    li = lax.broadcasted_iota(jnp.int32, (CHUNK, CHUNK), 0)
    si = lax.broadcasted_iota(jnp.int32, (CHUNK, CHUNK), 1)
    lo_half = si < SSD_HEAD_DIM
    lo_half2 = jnp.concatenate([lo_half, lo_half], axis=1)

    def expand(bc):
        return jnp.concatenate([jnp.where(lo_half, bc[0], bc[1]), jnp.where(lo_half, bc[2], bc[3])], axis=1)

    def column(tile, k):
        return jnp.broadcast_to(tile[:, k:k + 1], (CHUNK, LANES))

    def halo_rows(src_ref, c):
        r0 = pl.multiple_of(c * CHUNK, CHUNK)
        cur = src_ref[0, pl.ds(r0, CHUNK), :]
        prev = src_ref[0, pl.ds(pl.multiple_of(jnp.maximum(r0 - H, 0), H), H), :]
        nxt = src_ref[0, pl.ds(pl.multiple_of(jnp.minimum(r0 + CHUNK, S - H), H), H), :]
        prev = jnp.where(c > 0, prev, jnp.zeros_like(prev))
        nxt = jnp.where(c < nc - 1, nxt, jnp.zeros_like(nxt))
        return jnp.concatenate([prev, cur, nxt], axis=0), cur

    def conv_body(c, carry):
        rows = pl.ds(pl.multiple_of(c * CHUNK, CHUNK), CHUNK)
        parts = [halo_rows(r, c) for r in (xs_ref, b_ref, c_ref)]
        cat = jnp.concatenate([p[0] for p in parts], axis=1)
        cur = jnp.concatenate([p[1] for p in parts], axis=1).astype(F32)
        sh = jnp.dot(shift_ref[...], cat, preferred_element_type=F32)
        w = cw_ref[0]
        acc = cb_ref[0] + w[2:3, :] * cur
        for j, k in enumerate((0, 1, 3, 4)):
            acc = acc + w[k:k + 1, :] * sh[j * CHUNK:(j + 1) * CHUNK, :]
        act = acc * _sigmoid(acc)
        xs = act[:, :GROUP_W]
        bm = act[:, GROUP_W:GROUP_W + N]
        xlo_s[rows, :] = jnp.where(lo_half2, xs, 0.0).astype(BF)
        xhi_s[rows, :] = jnp.where(lo_half2, 0.0, xs).astype(BF)
        cm_s[rows, :] = act[:, GROUP_W + N:].astype(BF)
        bt_s[:, rows] = bm.T.astype(BF)
        col = col_ref[0, 0, rows, :]
        xdf_s[rows, :] = (xs * expand([column(col, r) for r in range(hpg)])).astype(BF)
        xdb_s[rows, :] = (xs * expand([column(col, hpg + r) for r in range(hpg)])).astype(BF)
        return carry

    lax.fori_loop(0, nc, conv_body, 0, unroll=4)

    h_s[...] = jnp.zeros_like(h_s)

    def chunk_step(c, d):
        rows = pl.ds(pl.multiple_of(c * CHUNK, CHUNK), CHUNK)
        bm_t = bt_s[:, rows]
        cm = cm_s[rows, :]
        col = col_ref[0, 0, rows, :]
        row = row_ref[0, 0, :, rows]
        o = hpg * d
        csb = [column(col, 2 * hpg + o + r) for r in range(hpg)]
        ecs = jnp.exp(expand(csb))
        causal = (li >= si) if d == 0 else (li <= si)
        cb = jnp.dot(cm, bm_t, preferred_element_type=F32)
        ms = []
        for r in range(hpg):
            diff = csb[r] - row[o + r:o + r + 1, :]
            decay = jnp.exp(jnp.where(causal, diff, NEG_BIG))
            ms.append((cb * decay * row[2 * hpg + o + r:2 * hpg + o + r + 1, :]).astype(BF))
        ys = []
        for p in range(2):
            lanes = slice(LANES * p, LANES * (p + 1))
            m2 = jnp.concatenate([ms[2 * p], ms[2 * p + 1]], axis=1)
            x2 = jnp.concatenate([xlo_s[rows, lanes], xhi_s[rows, lanes]], axis=0)
            ys.append(jnp.dot(m2, x2, preferred_element_type=F32))
        hprev = h_s[d]
        yoff = jnp.dot(cm, hprev.astype(BF), preferred_element_type=F32) * ecs
        edge = ecs[CHUNK - 1:CHUNK, :] if d == 0 else ecs[0:1, :]
        xdec = xdf_s[rows, :] if d == 0 else xdb_s[rows, :]
        h_s[d] = hprev * edge + jnp.dot(bm_t, xdec, preferred_element_type=F32)
        y = jnp.concatenate(ys, axis=1) + yoff
        if d == 0:
            yf_s[rows, :] = y
        else:
            yb_s[rows, :] = y

    def scan_body(i, carry):
        chunk_step(i, 0)
        chunk_step(nc - 1 - i, 1)
        return carry

    lax.fori_loop(0, nc, scan_body, 0, unroll=4)

    def out_body(c, carry):
        rows = pl.ds(pl.multiple_of(c * CHUNK, CHUNK), CHUNK)
        xs = (xlo_s[rows, :] + xhi_s[rows, :]).astype(F32)
        y = yf_s[rows, :] + yb_s[rows, :] + dsk_ref[0] * xs
        z = z_ref[0, rows, :].astype(F32)
        y = y * (z * _sigmoid(z))
        y_ref[0, rows, :] = _rms(y, nw_ref[0]).astype(BF)
        return carry

    lax.fori_loop(0, nc, out_body, 0, unroll=4)


def _shift_matrices():
    t = np.arange(CHUNK)[:, None]
    j = np.arange(CHUNK + 2 * CONV_HALO)[None, :]
    return np.concatenate([(j == t + CONV_HALO + k - 2) for k in (0, 1, 3, 4)], axis=0).astype(np.float32)


def _ssd(proj3, col_tbl, row_tbl, conv_w_g, conv_b_g, dskip_g, normw_g):
    B, S, _ = proj3.shape
    G, N, W = SSD_GROUPS, SSD_STATE, GROUP_W
    xs_blk = COL_XBC // W
    b_blk = (COL_XBC + SSD_D_INNER) // N
    c_blk = b_blk + G
    shift = jnp.asarray(_shift_matrices(), BF)
    return pl.pallas_call(
        _ssd_kernel,
        out_shape=jax.ShapeDtypeStruct((B, S, SSD_D_INNER), BF),
        grid=(B, G),
        in_specs=[
            pl.BlockSpec((1, S, W), lambda b, g: (b, 0, xs_blk + g)),
            pl.BlockSpec((1, S, N), lambda b, g: (b, 0, b_blk + g)),
            pl.BlockSpec((1, S, N), lambda b, g: (b, 0, c_blk + g)),
            pl.BlockSpec((1, S, W), lambda b, g: (b, 0, g)),
            pl.BlockSpec((1, 1, S, LANES), lambda b, g: (b, g, 0, 0)),
            pl.BlockSpec((1, 1, 4 * SSD_HPG, S), lambda b, g: (b, g, 0, 0)),
            pl.BlockSpec((1, 8, W + 2 * N), lambda b, g: (g, 0, 0)),
            pl.BlockSpec((1, 1, W + 2 * N), lambda b, g: (g, 0, 0)),
            pl.BlockSpec(shift.shape, lambda b, g: (0, 0)),
            pl.BlockSpec((1, 1, W), lambda b, g: (g, 0, 0)),
            pl.BlockSpec((1, 1, W), lambda b, g: (g, 0, 0)),
        ],
        out_specs=pl.BlockSpec((1, S, W), lambda b, g: (b, 0, g)),
        scratch_shapes=[
            pltpu.VMEM((S, W), BF),
            pltpu.VMEM((S, W), BF),
            pltpu.VMEM((S, N), BF),
            pltpu.VMEM((N, S), BF),
            pltpu.VMEM((S, W), BF),
            pltpu.VMEM((S, W), BF),
            pltpu.VMEM((S, W), F32),
            pltpu.VMEM((S, W), F32),
            pltpu.VMEM((2, N, W), F32),
        ],
        compiler_params=_cparams(("parallel", "parallel")),
        name="ssd",
    )(proj3, proj3, proj3, proj3, col_tbl, row_tbl, conv_w_g, conv_b_g, shift, dskip_g, normw_g)


def _merge_kernel(seq_len, ys_ref, xp_ref, xprev_ref, xnext_ref, gate1_ref, gate2_ref, x_ref,
                  wssd_ref, band_ref, wg_ref, pscale_ref, wpool_ref, wout_ref, fnw_ref,
                  h_ref, hnt_ref):
    tm = ys_ref.shape[0]
    i = pl.program_id(0)
    pos0 = (i * tm) % seq_len
    yssd = jnp.dot(ys_ref[...], wssd_ref[...], preferred_element_type=F32)

    cur = xp_ref[...]
    zero = jnp.zeros((POOL_HALO, POOL_WIDTH), BF)
    prev = jnp.where(pos0 > 0, xprev_ref[...], zero)
    nxt = jnp.where(pos0 + tm < seq_len, xnext_ref[...], zero)
    cat = jnp.concatenate([prev, cur, nxt], axis=0)
    pos = pos0 + lax.broadcasted_iota(jnp.int32, (tm, 1), 0)
    pooled = []
    for gi, w in enumerate(POOL_WINDOWS):
        cols = slice(gi * POOL_GDIM, (gi + 1) * POOL_GDIM)
        wsum = jnp.dot(band_ref[gi], cat[:, cols], preferred_element_type=F32)
        cnt = jnp.minimum(pos + w // 2, seq_len) - jnp.maximum(pos - w // 2, 0)
        pg = wsum / cnt.astype(F32) - cur[:, cols].astype(F32)
        pg = jnp.dot(pg.astype(BF), wg_ref[gi], preferred_element_type=F32)
        pooled.append((pg * pscale_ref[:, cols]).astype(BF))
    ypool = jnp.dot(jnp.concatenate(pooled, axis=1), wpool_ref[...], preferred_element_type=F32)

    merged = _sigmoid(gate1_ref[...].astype(F32)) * yssd + _sigmoid(gate2_ref[...].astype(F32)) * ypool
    h = x_ref[...] + jnp.dot(merged.astype(BF), wout_ref[...], preferred_element_type=F32)
    h_ref[...] = h
    hnt_ref[...] = _rms(h, fnw_ref[...]).T.astype(BF)


def _merge(y_ssd, proj, x2, w_ssd, band, w_g, pool_scale, w_pool, w_out, ffn_w, seq_len, tm=256):
    T, D = x2.shape
    hb = tm // POOL_HALO
    last_hb = T // POOL_HALO - 1
    pool_blk = COL_POOL // POOL_WIDTH
    g1_blk = COL_GATE // D
    return pl.pallas_call(
        functools.partial(_merge_kernel, seq_len),
        out_shape=(jax.ShapeDtypeStruct((T, D), F32), jax.ShapeDtypeStruct((D, T), BF)),
        grid=(T // tm,),
        in_specs=[
            pl.BlockSpec((tm, D), lambda i: (i, 0)),
            pl.BlockSpec((tm, POOL_WIDTH), lambda i: (i, pool_blk)),
            pl.BlockSpec((POOL_HALO, POOL_WIDTH), lambda i: (jnp.maximum(i * hb - 1, 0), pool_blk)),
            pl.BlockSpec((POOL_HALO, POOL_WIDTH), lambda i: (jnp.minimum((i + 1) * hb, last_hb), pool_blk)),
            pl.BlockSpec((tm, D), lambda i: (i, g1_blk)),
            pl.BlockSpec((tm, D), lambda i: (i, g1_blk + 1)),
            pl.BlockSpec((tm, D), lambda i: (i, 0)),
            _resident(w_ssd.shape),
            _resident(band.shape),
            _resident(w_g.shape),
            _resident(pool_scale.shape),
            _resident(w_pool.shape),
            _resident(w_out.shape),
            _resident(ffn_w.shape),
        ],
        out_specs=(
            pl.BlockSpec((tm, D), lambda i: (i, 0)),
            pl.BlockSpec((D, tm), lambda i: (0, i)),
        ),
        compiler_params=_cparams(("parallel",)),
        name="merge",
    )(y_ssd, proj, proj, proj, proj, proj, x2, w_ssd, band, w_g, pool_scale, w_pool, w_out, ffn_w)


def _cmpx(v, i, j):
    hi = jnp.maximum(v[i], v[j])
    lo = jnp.minimum(v[i], v[j])
    v[i], v[j] = hi, lo


def _bitonic_merge16(v):
    v = list(v)
    for j in (8, 4, 2, 1):
        for i in range(16):
            if i ^ j > i:
                _cmpx(v, i, i ^ j)
    return v


def _bitonic_sort16(v):
    v = list(v)
    for k in (2, 4, 8):
        for j in [s for s in (4, 2, 1) if s < k]:
            for i in range(16):
                l = i ^ j
                if l > i:
                    if i & k == 0:
                        _cmpx(v, i, l)
                    else:
                        _cmpx(v, l, i)
    return _bitonic_merge16(v)


def _top16_union(p, q):
    return _bitonic_merge16([jnp.maximum(p[i], q[15 - i]) for i in range(16)])


def _prefix_count(bs, pred):
    m8 = pred(bs[7])
    m4 = pred(jnp.where(m8, bs[11], bs[3]))
    m2 = pred(jnp.where(m8, jnp.where(m4, bs[13], bs[9]), jnp.where(m4, bs[5], bs[1])))
    lo = jnp.where(m4, jnp.where(m2, bs[6], bs[4]), jnp.where(m2, bs[2], bs[0]))
    hi = jnp.where(m4, jnp.where(m2, bs[14], bs[12]), jnp.where(m2, bs[10], bs[8]))
    m1 = pred(jnp.where(m8, hi, lo))
    cnt = (jnp.where(m8, 8.0, 0.0) + jnp.where(m4, 4.0, 0.0)) + (jnp.where(m2, 2.0, 0.0) + jnp.where(m1, 1.0, 0.0))
    return jnp.where(pred(bs[15]), 16.0, cnt)


def _route_kernel(hnt_ref, wq_ref, keys_ref, r1_ref, vh_ref, n_ref, u_ref, q_s, sc_s, ac_s, bc_s, tz_s):
    tm = hnt_ref.shape[1]
    nlb = tm // LANES
    K = PEER_TOPK
    qrows = 4 * PEER_HALF

    for r0 in range(0, q_s.shape[0], qrows):
        q_s[r0:r0 + qrows, :] = jnp.dot(wq_ref[r0:r0 + qrows, :], hnt_ref[...],
                                        preferred_element_type=F32).astype(BF)
    for hk in range(2 * PEER_HEADS):
        sc_s[hk] = jnp.dot(keys_ref[hk], q_s[hk * PEER_HALF:(hk + 1) * PEER_HALF, :],
                           preferred_element_type=F32)
    ac_s[...] = jnp.zeros_like(ac_s)
    bc_s[...] = jnp.zeros_like(bc_s)

    sub = lax.broadcasted_iota(jnp.int32, (8, LANES), 0)

    def top_body(idx, carry):
        h = idx // nlb
        lanes = pl.ds(pl.multiple_of((idx % nlb) * LANES, LANES), LANES)
        for half, dst in ((0, ac_s), (1, bc_s)):
            v = _bitonic_sort16([sc_s[2 * h + half, pl.ds(8 * k, 8), lanes] for k in range(PEER_KEYS // 8)])
            for sh in (4, 2, 1):
                v = _bitonic_merge16([jnp.maximum(v[i], pltpu.roll(v[15 - i], sh, axis=0)) for i in range(16)])
            for r in range(K):
                dst[r, :, lanes] = jnp.where(sub == h, v[r], dst[r, :, lanes])
        return carry

    lax.fori_loop(0, PEER_HEADS * nlb, top_body, 0)

    def pair_body(lb, carry):
        lanes = pl.ds(pl.multiple_of(lb * LANES, LANES), LANES)
        a = [ac_s[r, :, lanes] for r in range(K)]
        b = [bc_s[r, :, lanes] for r in range(K)]
        first = [a[0] + b[j] for j in range(K)]
        rest = [a[i] + b[j] for i in range(1, K) for j in range(K // (i + 1))]
        rest = rest + [jnp.full_like(a[0], -jnp.inf)] * (3 * K - len(rest))
        groups = [_bitonic_sort16(rest[K * g:K * (g + 1)]) for g in range(3)]
        top = _top16_union(_top16_union(first, groups[0]), _top16_union(groups[1], groups[2]))
        z = None
        for r in range(K):
            e = jnp.exp(top[r] - top[0])
            z = e if z is None else z + e
        tz_s[0, :, lanes] = top[K - 1]
        tz_s[1, :, lanes] = 1.0 / z
        return carry

    lax.fori_loop(0, nlb, pair_body, 0)

    def table_body(lb, carry):
        lanes = pl.ds(pl.multiple_of(lb * LANES, LANES), LANES)
        for h in range(PEER_HEADS):
            s0 = sc_s[2 * h, :, lanes]
            s1 = sc_s[2 * h + 1, :, lanes]
            tau = tz_s[0, h:h + 1, lanes]
            bs = [bc_s[r, h:h + 1, lanes] for r in range(K)]
            r1_ref[h, :, lanes] = (_prefix_count(bs, lambda b: b > s1) + 1.0).astype(BF)
            n_ref[h, :, lanes] = _prefix_count(bs, lambda b: s0 + b >= tau)
            vh_ref[h, :, lanes] = jnp.exp(s1 - bc_s[0, h:h + 1, lanes]).astype(BF)
            u_ref[h, :, lanes] = jnp.exp(s0 - ac_s[0, h:h + 1, lanes]) * tz_s[1, h:h + 1, lanes]
        return carry

    lax.fori_loop(0, nlb, table_body, 0)


def _route(hnt, wq_t, keys, tm=512):
    D, T = hnt.shape
    H, NK = PEER_HEADS, PEER_KEYS
    tbl = lambda dt: jax.ShapeDtypeStruct((H, NK, T), dt)
    tspec = pl.BlockSpec((H, NK, tm), lambda i: (0, 0, i))
    return pl.pallas_call(
        _route_kernel,
        out_shape=(tbl(BF), tbl(BF), tbl(F32), tbl(F32)),
        grid=(T // tm,),
        in_specs=[
            pl.BlockSpec((D, tm), lambda i: (0, i)),
            _resident(wq_t.shape),
            _resident(keys.shape),
        ],
        out_specs=(tspec, tspec, tspec, tspec),
        scratch_shapes=[
            pltpu.VMEM((wq_t.shape[0], tm), BF),
            pltpu.VMEM((2 * H, NK, tm), F32),
            pltpu.VMEM((PEER_TOPK, H, tm), F32),
            pltpu.VMEM((PEER_TOPK, H, tm), F32),
            pltpu.VMEM((2, H, tm), F32),
        ],
        compiler_params=_cparams(("parallel",)),
        name="route",
    )(hnt, wq_t, keys)


def _experts_kernel(hnt_ref, u_ref, vt_ref, r1_ref, vh_ref, n_ref, uu_ref, h_ref, fw_ref,
                    o_ref, acc_s, coef_s):
    eb = u_ref.shape[0]
    tm = hnt_ref.shape[1]
    e = pl.program_id(1)

    @pl.when(e == 0)
    def _():
        acc_s[...] = jnp.zeros_like(acc_s)

    a_t = jnp.dot(u_ref[...], hnt_ref[...], preferred_element_type=F32)
    for ib in range(eb // PEER_KEYS):
        a = a_t[ib * PEER_KEYS:(ib + 1) * PEER_KEYS, :]
        act = (0.5 * a * (1.0 + lax.erf(a * math.sqrt(0.5)))).astype(BF)
        gate = jnp.zeros((PEER_KEYS, tm), BF)
        for h in range(PEER_HEADS):
            nrow = jnp.broadcast_to(n_ref[h, ib:ib + 1, :], (PEER_KEYS, tm)).astype(BF)
            urow = jnp.broadcast_to(uu_ref[h, ib:ib + 1, :], (PEER_KEYS, tm)).astype(BF)
            sel = jnp.where(r1_ref[h] <= nrow, vh_ref[h], jnp.zeros((PEER_KEYS, tm), BF))
            gate = gate + sel * urow
        coef_s[ib * PEER_KEYS:(ib + 1) * PEER_KEYS, :] = act * gate
    acc_s[...] += jnp.dot(vt_ref[...], coef_s[...], preferred_element_type=F32)

    @pl.when(e == pl.num_programs(1) - 1)
    def _():
        o_ref[...] = _rms(h_ref[...] + acc_s[...].T, fw_ref[...])


def _experts(hnt, u_b, v_t, r1, vh, n, uu, h, final_w, tm=512):
    D, T = hnt.shape
    E = u_b.shape[0]
    H, NK = PEER_HEADS, PEER_KEYS
    kb = 8
    eb = kb * NK
    tspec = pl.BlockSpec((H, NK, tm), lambda i, e: (0, 0, i), pipeline_mode=pl.Buffered(1))
    kspec = pl.BlockSpec((H, kb, tm), lambda i, e: (0, e, i))
    return pl.pallas_call(
        _experts_kernel,
        out_shape=jax.ShapeDtypeStruct((T, D), F32),
        grid=(T // tm, E // eb),
        in_specs=[
            pl.BlockSpec((D, tm), lambda i, e: (0, i), pipeline_mode=pl.Buffered(1)),
            pl.BlockSpec((eb, D), lambda i, e: (e, 0)),
            pl.BlockSpec((D, eb), lambda i, e: (0, e)),
            tspec, tspec, kspec, kspec,
            pl.BlockSpec((tm, D), lambda i, e: (i, 0), pipeline_mode=pl.Buffered(1)),
            pl.BlockSpec((1, D), lambda i, e: (0, 0)),
        ],
        out_specs=pl.BlockSpec((tm, D), lambda i, e: (i, 0)),
        scratch_shapes=[pltpu.VMEM((D, tm), F32), pltpu.VMEM((eb, tm), BF)],
        compiler_params=_cparams(("parallel", "arbitrary")),
        name="experts",
    )(hnt, u_b, v_t, r1, vh, n, uu, h, final_w)


def _band_matrices(tm):
    t = np.arange(tm)[:, None]
    j = np.arange(tm + 2 * POOL_HALO)[None, :]
    return np.stack([((j >= t + POOL_HALO - w // 2) & (j < t + POOL_HALO + w // 2)) for w in POOL_WINDOWS]
                    ).astype(np.float32)


def _tri_matrix():
    l = np.arange(CHUNK)[:, None]
    s = np.arange(CHUNK)[None, :]
    return np.concatenate([(s <= l), (s >= l)], axis=0).astype(np.float32)


def _dt_perm():
    cols = []
    for g in range(SSD_GROUPS):
        for d in range(2):
            for r in range(SSD_HPG):
                cols.append(d * SSD_HEADS + g * SSD_HPG + r)
    return np.asarray(cols)


def kernel(x, mixer_norm_w, w_in, conv_w, conv_b, dt_bias, a_log, d_skip, ssd_norm_w, w_ssd_branch, w_pool_group, pool_scale, w_pool_branch, w_out, ffn_norm_w, w_query, sub_keys, expert_u, expert_v, final_norm_w):
    return _forward(x, mixer_norm_w, w_in, conv_w, conv_b, dt_bias, a_log, d_skip, ssd_norm_w, w_ssd_branch,
                    w_pool_group, pool_scale, w_pool_branch, w_out, ffn_norm_w, w_query, sub_keys, expert_u,
                    expert_v, final_norm_w)[0]


def _forward(x, mixer_norm_w, w_in, conv_w, conv_b, dt_bias, a_log, d_skip, ssd_norm_w, w_ssd_branch, w_pool_group, pool_scale, w_pool_branch, w_out, ffn_norm_w, w_query, sub_keys, expert_u, expert_v, final_norm_w):
    B, S, D = x.shape
    T = B * S
    G, N, W = SSD_GROUPS, SSD_STATE, GROUP_W
    assert D == D_MODEL and mixer_norm_w.shape[0] == 1 and S % 512 == 0
    merge_tm = 256

    w_in0 = w_in[0]
    o1 = SSD_D_INNER
    o2 = o1 + SSD_XBC
    o3 = o2 + 2 * SSD_HEADS
    o4 = o3 + POOL_WIDTH
    w_main = jnp.concatenate([w_in0[:, :o2], w_in0[:, o4:], w_in0[:, o3:o4]], axis=1).astype(BF)
    perm = _dt_perm()
    pad = LANES - perm.size
    w_dt = jnp.pad(w_in0[:, o2:o3][:, perm], ((0, 0), (0, pad))).astype(BF)
    bias_row = jnp.pad(dt_bias[0].reshape(-1)[perm], (0, pad)).reshape(1, LANES)
    alog_row = jnp.pad(a_log[0].reshape(-1)[perm], (0, pad)).reshape(1, LANES)

    cw = conv_w[0, :, 0, :]
    cwx = cw[:, :SSD_D_INNER].reshape(SSD_CONV, G, W)
    cwb = cw[:, SSD_D_INNER:SSD_D_INNER + G * N].reshape(SSD_CONV, G, N)
    cwc = cw[:, SSD_D_INNER + G * N:].reshape(SSD_CONV, G, N)
    conv_w_g = jnp.pad(jnp.transpose(jnp.concatenate([cwx, cwb, cwc], axis=2), (1, 0, 2)),
                       ((0, 0), (0, 8 - SSD_CONV), (0, 0)))
    cbv = conv_b[0]
    conv_b_g = jnp.concatenate([cbv[:SSD_D_INNER].reshape(G, 1, W),
                                cbv[SSD_D_INNER:SSD_D_INNER + G * N].reshape(G, 1, N),
                                cbv[SSD_D_INNER + G * N:].reshape(G, 1, N)], axis=2)
    dskip_g = jnp.repeat(d_skip[0], SSD_HEAD_DIM).reshape(G, 1, W)
    normw_g = ssd_norm_w[0].reshape(G, 1, W)

    wq_t = w_query[0].T.astype(BF)
    keys = sub_keys[0].reshape(2 * PEER_HEADS, PEER_KEYS, PEER_HALF).astype(BF)
    u_b = expert_u[0].astype(BF)
    v_t = expert_v[0].T.astype(BF)

    x2 = x.reshape(T, D)
    proj, dt_raw = _in_proj(x2, mixer_norm_w, w_main, w_dt)
    dtcs, cst = _dt_prep(dt_raw.reshape(B, S, LANES), bias_row, alog_row, jnp.asarray(_tri_matrix(), BF))
    y_ssd = _ssd(proj.reshape(B, S, MAIN_W), dtcs, cst, conv_w_g, conv_b_g, dskip_g, normw_g)
    h, hnt = _merge(y_ssd.reshape(T, SSD_D_INNER), proj, x2,
                    w_ssd_branch[0].astype(BF), jnp.asarray(_band_matrices(merge_tm), BF),
                    w_pool_group[0].astype(BF), pool_scale, w_pool_branch[0].astype(BF),
                    w_out[0].astype(BF), ffn_norm_w, S, tm=merge_tm)
    r1, vh, n, uu = _route(hnt, wq_t, keys)
    out = _experts(hnt, u_b, v_t, r1, vh, n, uu, h, final_norm_w.reshape(1, D))
    return out.reshape(B, S, D), h
```

```python
import functools
import math

import numpy as np
import jax
import jax.numpy as jnp
from jax import lax
from jax.experimental import pallas as pl
from jax.experimental.pallas import tpu as pltpu

F32 = jnp.float32
BF = jnp.bfloat16

LANES = 128
V7X_VMEM_BYTES = 64 * 1024 * 1024
VMEM_LIMIT = 56 * 1024 * 1024

D_MODEL = 2048
SSD_HEAD_DIM = 64
SSD_D_INNER = D_MODEL
SSD_HEADS = SSD_D_INNER // SSD_HEAD_DIM
SSD_GROUPS = 8
SSD_HPG = SSD_HEADS // SSD_GROUPS
SSD_STATE = 128
SSD_CONV = 5
CHUNK = 128
SSD_XBC = SSD_D_INNER + 2 * SSD_GROUPS * SSD_STATE
GROUP_W = SSD_HPG * SSD_HEAD_DIM
POOL_WIDTH = D_MODEL // 2
POOL_WINDOWS = (2, 4, 8, 16)
POOL_GDIM = POOL_WIDTH // len(POOL_WINDOWS)
POOL_HALO = 64
CONV_HALO = 64
PEER_HEADS = 8
PEER_KEYS = 128
PEER_EXPERTS = PEER_KEYS * PEER_KEYS
PEER_HALF = 128
PEER_TOPK = 16
EPS = 1e-6
NEG_BIG = -1e30

MAIN_W = SSD_D_INNER + SSD_XBC + 2 * D_MODEL + POOL_WIDTH
COL_Z = 0
COL_XBC = SSD_D_INNER
COL_GATE = COL_XBC + SSD_XBC
COL_POOL = COL_GATE + 2 * D_MODEL


def _sigmoid(v):
    return 1.0 / (1.0 + jnp.exp(-v))


def _rms(v, w):
    return v * lax.rsqrt(jnp.mean(v * v, axis=-1, keepdims=True) + EPS) * w


def _cparams(sem):
    return pltpu.CompilerParams(dimension_semantics=sem, vmem_limit_bytes=VMEM_LIMIT)


def _resident(shape):
    nd = len(shape)
    return pl.BlockSpec(shape, lambda *_: (0,) * nd, pipeline_mode=pl.Buffered(1))


def _inproj_kernel(x_ref, nw_ref, w_ref, wdt_ref, o_ref, dt_ref, xn_ref):
    tm = x_ref.shape[0]

    @pl.when(pl.program_id(1) == 0)
    def _():
        def body(r, carry):
            rows = pl.ds(pl.multiple_of(r * CHUNK, CHUNK), CHUNK)
            xn = _rms(x_ref[rows, :], nw_ref[...]).astype(BF)
            xn_ref[rows, :] = xn
            dt_ref[rows, :] = jnp.dot(xn, wdt_ref[...], preferred_element_type=F32)
            return carry

        lax.fori_loop(0, tm // CHUNK, body, 0)

    o_ref[...] = jnp.dot(xn_ref[...], w_ref[...], preferred_element_type=F32).astype(BF)


def _in_proj(x2, norm_w, w_main, w_dt, tm=1024, tn=1024):
    T, D = x2.shape
    N = w_main.shape[1]
    return pl.pallas_call(
        _inproj_kernel,
        out_shape=(jax.ShapeDtypeStruct((T, N), BF), jax.ShapeDtypeStruct((T, LANES), F32)),
        grid=(T // tm, N // tn),
        in_specs=[
            pl.BlockSpec((tm, D), lambda i, j: (i, 0)),
            pl.BlockSpec((1, D), lambda i, j: (0, 0)),
            pl.BlockSpec((D, tn), lambda i, j: (0, j)),
            pl.BlockSpec((D, LANES), lambda i, j: (0, 0)),
        ],
        out_specs=(
            pl.BlockSpec((tm, tn), lambda i, j: (i, j)),
            pl.BlockSpec((tm, LANES), lambda i, j: (i, 0)),
        ),
        scratch_shapes=[pltpu.VMEM((tm, D), BF)],
        compiler_params=_cparams(("parallel", "arbitrary")),
        name="in_proj",
    )(x2, norm_w, w_main, w_dt)


def _split3(a):
    hi = a.astype(BF)
    r1 = a - hi.astype(F32)
    mid = r1.astype(BF)
    lo = (r1 - mid.astype(F32)).astype(BF)
    return hi, mid, lo


def _dtprep_kernel(raw_ref, bias_ref, alog_ref, tri_ref, col_ref, row_ref):
    nh = 2 * SSD_HPG
    lane = lax.broadcasted_iota(jnp.int32, (CHUNK, LANES), 1)
    backward = (lane % nh) >= SSD_HPG
    tri = tri_ref[...]
    for ci in range(raw_ref.shape[0] // CHUNK):
        rows = slice(ci * CHUNK, (ci + 1) * CHUNK)
        v = raw_ref[rows, :] + bias_ref[...]
        dt = jnp.maximum(v, 0.0) + jnp.log(1.0 + jnp.exp(-jnp.abs(v)))
        a = dt * (-jnp.exp(alog_ref[...]))
        cs2 = None
        for part in _split3(a):
            t = jnp.dot(tri, part, preferred_element_type=F32)
            cs2 = t if cs2 is None else cs2 + t
        cs = jnp.where(backward, cs2[CHUNK:, :], cs2[:CHUNK, :])
        edge = jnp.where(backward[0:1, :], cs[0:1, :], cs[CHUNK - 1:CHUNK, :])
        w = dt * jnp.exp(edge - cs)
        cs_t = cs.T
        dt_t = dt.T
        for g in range(SSD_GROUPS):
            wg = w if g == 0 else pltpu.roll(w, (LANES - nh * g) % LANES, axis=1)
            cg = pltpu.roll(cs, (LANES + nh - nh * g) % LANES, axis=1)
            col_ref[0, g, rows, :] = jnp.where(lane < nh, wg, jnp.where(lane < 2 * nh, cg, 0.0))
            row_ref[0, g, 0:nh, rows] = cs_t[nh * g:nh * (g + 1), :]
            row_ref[0, g, nh:2 * nh, rows] = dt_t[nh * g:nh * (g + 1), :]


def _dt_prep(dt_raw3, bias_row, alog_row, tri, rows=4 * CHUNK):
    B, S, _ = dt_raw3.shape
    nh = 2 * SSD_HPG
    return pl.pallas_call(
        _dtprep_kernel,
        out_shape=(jax.ShapeDtypeStruct((B, SSD_GROUPS, S, LANES), F32),
                   jax.ShapeDtypeStruct((B, SSD_GROUPS, 2 * nh, S), F32)),
        grid=(B, S // rows),
        in_specs=[
            pl.BlockSpec((None, rows, LANES), lambda b, c: (b, c, 0)),
            pl.BlockSpec((1, LANES), lambda b, c: (0, 0)),
            pl.BlockSpec((1, LANES), lambda b, c: (0, 0)),
            pl.BlockSpec((2 * CHUNK, CHUNK), lambda b, c: (0, 0)),
        ],
        out_specs=(
            pl.BlockSpec((1, SSD_GROUPS, rows, LANES), lambda b, c: (b, 0, c, 0)),
            pl.BlockSpec((1, SSD_GROUPS, 2 * nh, rows), lambda b, c: (b, 0, 0, c)),
        ),
        compiler_params=_cparams(("parallel", "parallel")),
        name="dt_prep",
    )(dt_raw3, bias_row, alog_row, tri)


def _ssd_kernel(xs_ref, b_ref, c_ref, z_ref, col_ref, row_ref, cw_ref, cb_ref, shift_ref, dsk_ref, nw_ref,
                y_ref, xlo_s, xhi_s, cm_s, bt_s, xdf_s, xdb_s, y_s, sf_s, sb_s, ef_s, eb_s, edge_s):
    S = xs_ref.shape[1]
    nc = S // CHUNK
    H = CONV_HALO
    N = SSD_STATE
    hpg = SSD_HPG

    li = lax.broadcasted_iota(jnp.int32, (CHUNK, CHUNK), 0)
    si = lax.broadcasted_iota(jnp.int32, (CHUNK, CHUNK), 1)
    lo_half = si < SSD_HEAD_DIM
    lo_half2 = jnp.concatenate([lo_half, lo_half], axis=1)

    def expand(bc):
        return jnp.concatenate([jnp.where(lo_half, bc[0], bc[1]), jnp.where(lo_half, bc[2], bc[3])], axis=1)

    def column(tile, k):
        return jnp.broadcast_to(tile[:, k:k + 1], (CHUNK, LANES))

    def halo_rows(src_ref, c):
        r0 = pl.multiple_of(c * CHUNK, CHUNK)
        cur = src_ref[0, pl.ds(r0, CHUNK), :]
        prev = src_ref[0, pl.ds(pl.multiple_of(jnp.maximum(r0 - H, 0), H), H), :]
        nxt = src_ref[0, pl.ds(pl.multiple_of(jnp.minimum(r0 + CHUNK, S - H), H), H), :]
        prev = jnp.where(c > 0, prev, jnp.zeros_like(prev))
        nxt = jnp.where(c < nc - 1, nxt, jnp.zeros_like(nxt))
        return jnp.concatenate([prev, cur, nxt], axis=0), cur

    def conv_body(c, carry):
        rows = pl.ds(pl.multiple_of(c * CHUNK, CHUNK), CHUNK)
        parts = [halo_rows(r, c) for r in (xs_ref, b_ref, c_ref)]
        cat = jnp.concatenate([p[0] for p in parts], axis=1)
        cur = jnp.concatenate([p[1] for p in parts], axis=1).astype(F32)
        sh = jnp.dot(shift_ref[...], cat, preferred_element_type=F32)
        w = cw_ref[0]
        acc = cb_ref[0] + w[2:3, :] * cur
        for j, k in enumerate((0, 1, 3, 4)):
            acc = acc + w[k:k + 1, :] * sh[j * CHUNK:(j + 1) * CHUNK, :]
        act = acc * _sigmoid(acc)
        xs = act[:, :GROUP_W]
        bm = act[:, GROUP_W:GROUP_W + N]
        xlo_s[rows, :] = jnp.where(lo_half2, xs, 0.0).astype(BF)
        xhi_s[rows, :] = jnp.where(lo_half2, 0.0, xs).astype(BF)
        cm_s[rows, :] = act[:, GROUP_W + N:].astype(BF)
        bt_s[:, rows] = bm.T.astype(BF)
        col = col_ref[0, 0, rows, :]
        xdf_s[rows, :] = (xs * expand([column(col, r) for r in range(hpg)])).astype(BF)
        xdb_s[rows, :] = (xs * expand([column(col, hpg + r) for r in range(hpg)])).astype(BF)
        return carry

    lax.fori_loop(0, nc, conv_body, 0, unroll=4)

    def diag_body(c, carry):
        rows = pl.ds(pl.multiple_of(c * CHUNK, CHUNK), CHUNK)
        bm_t = bt_s[:, rows]
        cm = cm_s[rows, :]
        col = col_ref[0, 0, rows, :]
        row = row_ref[0, 0, :, rows]
        cb = jnp.dot(cm, bm_t, preferred_element_type=F32)
        csb_f = [column(col, 2 * hpg + r) for r in range(hpg)]
        csb_b = [column(col, 3 * hpg + r) for r in range(hpg)]
        ms = []
        for r in range(hpg):
            wf = (jnp.exp(jnp.where(li >= si, csb_f[r] - row[r:r + 1, :], NEG_BIG))
                  * row[2 * hpg + r:2 * hpg + r + 1, :])
            wb = (jnp.exp(jnp.where(li <= si, csb_b[r] - row[hpg + r:hpg + r + 1, :], NEG_BIG))
                  * row[3 * hpg + r:3 * hpg + r + 1, :])
            ms.append((cb * (wf + wb)).astype(BF))
        ys = []
        for p in range(2):
            lanes = slice(LANES * p, LANES * (p + 1))
            m2 = jnp.concatenate([ms[2 * p], ms[2 * p + 1]], axis=1)
            x2 = jnp.concatenate([xlo_s[rows, lanes], xhi_s[rows, lanes]], axis=0)
            ys.append(jnp.dot(m2, x2, preferred_element_type=F32))
        y_s[rows, :] = jnp.concatenate(ys, axis=1)
        sf_s[c] = jnp.dot(bm_t, xdf_s[rows, :], preferred_element_type=F32)
        sb_s[c] = jnp.dot(bm_t, xdb_s[rows, :], preferred_element_type=F32)
        ecs_f = jnp.exp(expand(csb_f))
        ecs_b = jnp.exp(expand(csb_b))
        ef_s[rows, :] = ecs_f.astype(BF)
        eb_s[rows, :] = ecs_b.astype(BF)
        edge_s[c, 0:1, :] = ecs_f[CHUNK - 1:CHUNK, :]
        edge_s[c, 1:2, :] = ecs_b[0:1, :]
        return carry

    lax.fori_loop(0, nc, diag_body, 0, unroll=4)

    def state_body(i, carry):
        hf, hb = carry
        j = nc - 1 - i
        s_f = sf_s[i]
        s_b = sb_s[j]
        sf_s[i] = hf
        sb_s[j] = hb
        return hf * edge_s[i, 0:1, :] + s_f, hb * edge_s[j, 1:2, :] + s_b

    zero_state = jnp.zeros((N, GROUP_W), F32)
    lax.fori_loop(0, nc, state_body, (zero_state, zero_state))

    def out_body(c, carry):
        rows = pl.ds(pl.multiple_of(c * CHUNK, CHUNK), CHUNK)
        cm = cm_s[rows, :]
        yoff = (jnp.dot(cm, sf_s[c].astype(BF), preferred_element_type=F32) * ef_s[rows, :].astype(F32)
                + jnp.dot(cm, sb_s[c].astype(BF), preferred_element_type=F32) * eb_s[rows, :].astype(F32))
        xs = (xlo_s[rows, :] + xhi_s[rows, :]).astype(F32)
        y = y_s[rows, :] + yoff + dsk_ref[0] * xs
        z = z_ref[0, rows, :].astype(F32)
        y = y * (z * _sigmoid(z))
        y_ref[0, rows, :] = _rms(y, nw_ref[0]).astype(BF)
        return carry

    lax.fori_loop(0, nc, out_body, 0, unroll=4)


def _shift_matrices():
    t = np.arange(CHUNK)[:, None]
    j = np.arange(CHUNK + 2 * CONV_HALO)[None, :]
    return np.concatenate([(j == t + CONV_HALO + k - 2) for k in (0, 1, 3, 4)], axis=0).astype(np.float32)


def _ssd(proj3, col_tbl, row_tbl, conv_w_g, conv_b_g, dskip_g, normw_g):
    B, S, _ = proj3.shape
    G, N, W = SSD_GROUPS, SSD_STATE, GROUP_W
    xs_blk = COL_XBC // W
    b_blk = (COL_XBC + SSD_D_INNER) // N
    c_blk = b_blk + G
    shift = jnp.asarray(_shift_matrices(), BF)
    return pl.pallas_call(
        _ssd_kernel,
        out_shape=jax.ShapeDtypeStruct((B, S, SSD_D_INNER), BF),
        grid=(B, G),
        in_specs=[
            pl.BlockSpec((1, S, W), lambda b, g: (b, 0, xs_blk + g)),
            pl.BlockSpec((1, S, N), lambda b, g: (b, 0, b_blk + g)),
            pl.BlockSpec((1, S, N), lambda b, g: (b, 0, c_blk + g)),
            pl.BlockSpec((1, S, W), lambda b, g: (b, 0, g)),
            pl.BlockSpec((1, 1, S, LANES), lambda b, g: (b, g, 0, 0)),
            pl.BlockSpec((1, 1, 4 * SSD_HPG, S), lambda b, g: (b, g, 0, 0)),
            pl.BlockSpec((1, 8, W + 2 * N), lambda b, g: (g, 0, 0)),
            pl.BlockSpec((1, 1, W + 2 * N), lambda b, g: (g, 0, 0)),
            pl.BlockSpec(shift.shape, lambda b, g: (0, 0)),
            pl.BlockSpec((1, 1, W), lambda b, g: (g, 0, 0)),
            pl.BlockSpec((1, 1, W), lambda b, g: (g, 0, 0)),
        ],
        out_specs=pl.BlockSpec((1, S, W), lambda b, g: (b, 0, g)),
        scratch_shapes=[
            pltpu.VMEM((S, W), BF),
            pltpu.VMEM((S, W), BF),
            pltpu.VMEM((S, N), BF),
            pltpu.VMEM((N, S), BF),
            pltpu.VMEM((S, W), BF),
            pltpu.VMEM((S, W), BF),
            pltpu.VMEM((S, W), F32),
            pltpu.VMEM((S // CHUNK, N, W), F32),
            pltpu.VMEM((S // CHUNK, N, W), F32),
            pltpu.VMEM((S, W), BF),
            pltpu.VMEM((S, W), BF),
            pltpu.VMEM((S // CHUNK, 8, W), F32),
        ],
        compiler_params=_cparams(("parallel", "parallel")),
        name="ssd",
    )(proj3, proj3, proj3, proj3, col_tbl, row_tbl, conv_w_g, conv_b_g, shift, dskip_g, normw_g)


def _merge_kernel(seq_len, ys_ref, xp_ref, xprev_ref, xnext_ref, gate1_ref, gate2_ref, x_ref,
                  wssd_ref, band_ref, wg_ref, pscale_ref, wpool_ref, wout_ref, fnw_ref,
                  h_ref, hnt_ref):
    tm = ys_ref.shape[0]
    i = pl.program_id(0)
    pos0 = (i * tm) % seq_len
    yssd = jnp.dot(ys_ref[...], wssd_ref[...], preferred_element_type=F32)

    cur = xp_ref[...]
    zero = jnp.zeros((POOL_HALO, POOL_WIDTH), BF)
    prev = jnp.where(pos0 > 0, xprev_ref[...], zero)
    nxt = jnp.where(pos0 + tm < seq_len, xnext_ref[...], zero)
    cat = jnp.concatenate([prev, cur, nxt], axis=0)
    pos = pos0 + lax.broadcasted_iota(jnp.int32, (tm, 1), 0)
    pooled = []
    for gi, w in enumerate(POOL_WINDOWS):
        cols = slice(gi * POOL_GDIM, (gi + 1) * POOL_GDIM)
        wsum = jnp.dot(band_ref[gi], cat[:, cols], preferred_element_type=F32)
        cnt = jnp.minimum(pos + w // 2, seq_len) - jnp.maximum(pos - w // 2, 0)
        pg = wsum / cnt.astype(F32) - cur[:, cols].astype(F32)
        pg = jnp.dot(pg.astype(BF), wg_ref[gi], preferred_element_type=F32)
        pooled.append((pg * pscale_ref[:, cols]).astype(BF))
    ypool = jnp.dot(jnp.concatenate(pooled, axis=1), wpool_ref[...], preferred_element_type=F32)

    merged = _sigmoid(gate1_ref[...].astype(F32)) * yssd + _sigmoid(gate2_ref[...].astype(F32)) * ypool
    h = x_ref[...] + jnp.dot(merged.astype(BF), wout_ref[...], preferred_element_type=F32)
    h_ref[...] = h
    hnt_ref[...] = _rms(h, fnw_ref[...]).T.astype(BF)


def _merge(y_ssd, proj, x2, w_ssd, band, w_g, pool_scale, w_pool, w_out, ffn_w, seq_len, tm=256):
    T, D = x2.shape
    hb = tm // POOL_HALO
    last_hb = T // POOL_HALO - 1
    pool_blk = COL_POOL // POOL_WIDTH
    g1_blk = COL_GATE // D
    return pl.pallas_call(
        functools.partial(_merge_kernel, seq_len),
        out_shape=(jax.ShapeDtypeStruct((T, D), F32), jax.ShapeDtypeStruct((D, T), BF)),
        grid=(T // tm,),
        in_specs=[
            pl.BlockSpec((tm, D), lambda i: (i, 0)),
            pl.BlockSpec((tm, POOL_WIDTH), lambda i: (i, pool_blk)),
            pl.BlockSpec((POOL_HALO, POOL_WIDTH), lambda i: (jnp.maximum(i * hb - 1, 0), pool_blk)),
            pl.BlockSpec((POOL_HALO, POOL_WIDTH), lambda i: (jnp.minimum((i + 1) * hb, last_hb), pool_blk)),
            pl.BlockSpec((tm, D), lambda i: (i, g1_blk)),
            pl.BlockSpec((tm, D), lambda i: (i, g1_blk + 1)),
            pl.BlockSpec((tm, D), lambda i: (i, 0)),
            _resident(w_ssd.shape),
            _resident(band.shape),
            _resident(w_g.shape),
            _resident(pool_scale.shape),
            _resident(w_pool.shape),
            _resident(w_out.shape),
            _resident(ffn_w.shape),
        ],
        out_specs=(
            pl.BlockSpec((tm, D), lambda i: (i, 0)),
            pl.BlockSpec((D, tm), lambda i: (0, i)),
        ),
        compiler_params=_cparams(("parallel",)),
        name="merge",
    )(y_ssd, proj, proj, proj, proj, proj, x2, w_ssd, band, w_g, pool_scale, w_pool, w_out, ffn_w)


def _cmpx(v, i, j):
    hi = jnp.maximum(v[i], v[j])
    lo = jnp.minimum(v[i], v[j])
    v[i], v[j] = hi, lo


def _bitonic_merge16(v):
    v = list(v)
    for j in (8, 4, 2, 1):
        for i in range(16):
            if i ^ j > i:
                _cmpx(v, i, i ^ j)
    return v


def _bitonic_sort16(v):
    v = list(v)
    for k in (2, 4, 8):
        for j in [s for s in (4, 2, 1) if s < k]:
            for i in range(16):
                l = i ^ j
                if l > i:
                    if i & k == 0:
                        _cmpx(v, i, l)
                    else:
                        _cmpx(v, l, i)
    return _bitonic_merge16(v)


def _top16_union(p, q):
    return _bitonic_merge16([jnp.maximum(p[i], q[15 - i]) for i in range(16)])


def _prefix_count(bs, pred):
    m8 = pred(bs[7])
    m4 = pred(jnp.where(m8, bs[11], bs[3]))
    m2 = pred(jnp.where(m8, jnp.where(m4, bs[13], bs[9]), jnp.where(m4, bs[5], bs[1])))
    lo = jnp.where(m4, jnp.where(m2, bs[6], bs[4]), jnp.where(m2, bs[2], bs[0]))
    hi = jnp.where(m4, jnp.where(m2, bs[14], bs[12]), jnp.where(m2, bs[10], bs[8]))
    m1 = pred(jnp.where(m8, hi, lo))
    cnt = (jnp.where(m8, 8.0, 0.0) + jnp.where(m4, 4.0, 0.0)) + (jnp.where(m2, 2.0, 0.0) + jnp.where(m1, 1.0, 0.0))
    return jnp.where(pred(bs[15]), 16.0, cnt)


def _route_kernel(hnt_ref, wq_ref, keys_ref, r1_ref, vh_ref, n_ref, u_ref, q_s, sc_s, ac_s, bc_s, tz_s):
    tm = hnt_ref.shape[1]
    nlb = tm // LANES
    K = PEER_TOPK
    qrows = 4 * PEER_HALF

    for r0 in range(0, q_s.shape[0], qrows):
        q_s[r0:r0 + qrows, :] = jnp.dot(wq_ref[r0:r0 + qrows, :], hnt_ref[...],
                                        preferred_element_type=F32).astype(BF)
    for hk in range(2 * PEER_HEADS):
        sc_s[hk] = jnp.dot(keys_ref[hk], q_s[hk * PEER_HALF:(hk + 1) * PEER_HALF, :],
                           preferred_element_type=F32)
    ac_s[...] = jnp.zeros_like(ac_s)
    bc_s[...] = jnp.zeros_like(bc_s)

    sub = lax.broadcasted_iota(jnp.int32, (8, LANES), 0)

    def top_body(idx, carry):
        h = idx // nlb
        lanes = pl.ds(pl.multiple_of((idx % nlb) * LANES, LANES), LANES)
        for half, dst in ((0, ac_s), (1, bc_s)):
            v = _bitonic_sort16([sc_s[2 * h + half, pl.ds(8 * k, 8), lanes] for k in range(PEER_KEYS // 8)])
            for sh in (4, 2, 1):
                v = _bitonic_merge16([jnp.maximum(v[i], pltpu.roll(v[15 - i], sh, axis=0)) for i in range(16)])
            for r in range(K):
                dst[r, :, lanes] = jnp.where(sub == h, v[r], dst[r, :, lanes])
        return carry

    lax.fori_loop(0, PEER_HEADS * nlb, top_body, 0)

    def pair_body(lb, carry):
        lanes = pl.ds(pl.multiple_of(lb * LANES, LANES), LANES)
        a = [ac_s[r, :, lanes] for r in range(K)]
        b = [bc_s[r, :, lanes] for r in range(K)]
        first = [a[0] + b[j] for j in range(K)]
        rest = [a[i] + b[j] for i in range(1, K) for j in range(K // (i + 1))]
        rest = rest + [jnp.full_like(a[0], -jnp.inf)] * (3 * K - len(rest))
        groups = [_bitonic_sort16(rest[K * g:K * (g + 1)]) for g in range(3)]
        top = _top16_union(_top16_union(first, groups[0]), _top16_union(groups[1], groups[2]))
        z = None
        for r in range(K):
            e = jnp.exp(top[r] - top[0])
            z = e if z is None else z + e
        tz_s[0, :, lanes] = top[K - 1]
        tz_s[1, :, lanes] = 1.0 / z
        return carry

    lax.fori_loop(0, nlb, pair_body, 0)

    def table_body(lb, carry):
        lanes = pl.ds(pl.multiple_of(lb * LANES, LANES), LANES)
        for h in range(PEER_HEADS):
            s0 = sc_s[2 * h, :, lanes]
            s1 = sc_s[2 * h + 1, :, lanes]
            tau = tz_s[0, h:h + 1, lanes]
            bs = [bc_s[r, h:h + 1, lanes] for r in range(K)]
            r1_ref[h, :, lanes] = (_prefix_count(bs, lambda b: b > s1) + 1.0).astype(BF)
            n_ref[h, :, lanes] = _prefix_count(bs, lambda b: s0 + b >= tau)
            vh_ref[h, :, lanes] = jnp.exp(s1 - bc_s[0, h:h + 1, lanes]).astype(BF)
            u_ref[h, :, lanes] = jnp.exp(s0 - ac_s[0, h:h + 1, lanes]) * tz_s[1, h:h + 1, lanes]
        return carry

    lax.fori_loop(0, nlb, table_body, 0)


def _route(hnt, wq_t, keys, tm=512):
    D, T = hnt.shape
    H, NK = PEER_HEADS, PEER_KEYS
    tbl = lambda dt: jax.ShapeDtypeStruct((H, NK, T), dt)
    tspec = pl.BlockSpec((H, NK, tm), lambda i: (0, 0, i))
    return pl.pallas_call(
        _route_kernel,
        out_shape=(tbl(BF), tbl(BF), tbl(F32), tbl(F32)),
        grid=(T // tm,),
        in_specs=[
            pl.BlockSpec((D, tm), lambda i: (0, i)),
            _resident(wq_t.shape),
            _resident(keys.shape),
        ],
        out_specs=(tspec, tspec, tspec, tspec),
        scratch_shapes=[
            pltpu.VMEM((wq_t.shape[0], tm), BF),
            pltpu.VMEM((2 * H, NK, tm), F32),
            pltpu.VMEM((PEER_TOPK, H, tm), F32),
            pltpu.VMEM((PEER_TOPK, H, tm), F32),
            pltpu.VMEM((2, H, tm), F32),
        ],
        compiler_params=_cparams(("parallel",)),
        name="route",
    )(hnt, wq_t, keys)


def _experts_kernel(hnt_ref, u_ref, vt_ref, r1_ref, vh_ref, n_ref, uu_ref, h_ref, fw_ref,
                    o_ref, acc_s, coef_s):
    eb = u_ref.shape[0]
    tm = hnt_ref.shape[1]
    e = pl.program_id(1)

    @pl.when(e == 0)
    def _():
        acc_s[...] = jnp.zeros_like(acc_s)

    a_t = jnp.dot(u_ref[...], hnt_ref[...], preferred_element_type=F32)
    for ib in range(eb // PEER_KEYS):
        a = a_t[ib * PEER_KEYS:(ib + 1) * PEER_KEYS, :]
        ab = a.astype(BF)
        act = (0.5 * ab) * (1.0 + lax.erf(ab * math.sqrt(0.5)))
        gate = jnp.zeros((PEER_KEYS, tm), BF)
        for h in range(PEER_HEADS):
            nrow = jnp.broadcast_to(n_ref[h, ib:ib + 1, :], (PEER_KEYS, tm)).astype(BF)
            urow = jnp.broadcast_to(uu_ref[h, ib:ib + 1, :], (PEER_KEYS, tm)).astype(BF)
            sel = jnp.where(r1_ref[h] <= nrow, vh_ref[h], jnp.zeros((PEER_KEYS, tm), BF))
            gate = gate + sel * urow
        coef_s[ib * PEER_KEYS:(ib + 1) * PEER_KEYS, :] = act * gate
    acc_s[...] += jnp.dot(vt_ref[...], coef_s[...], preferred_element_type=F32)

    @pl.when(e == pl.num_programs(1) - 1)
    def _():
        o_ref[...] = _rms(h_ref[...] + acc_s[...].T, fw_ref[...])


def _experts(hnt, u_b, v_t, r1, vh, n, uu, h, final_w, tm=512):
    D, T = hnt.shape
    E = u_b.shape[0]
    H, NK = PEER_HEADS, PEER_KEYS
    kb = 8
    eb = kb * NK
    tspec = pl.BlockSpec((H, NK, tm), lambda i, e: (0, 0, i), pipeline_mode=pl.Buffered(1))
    kspec = pl.BlockSpec((H, kb, tm), lambda i, e: (0, e, i))
    return pl.pallas_call(
        _experts_kernel,
        out_shape=jax.ShapeDtypeStruct((T, D), F32),
        grid=(T // tm, E // eb),
        in_specs=[
            pl.BlockSpec((D, tm), lambda i, e: (0, i), pipeline_mode=pl.Buffered(1)),
            pl.BlockSpec((eb, D), lambda i, e: (e, 0)),
            pl.BlockSpec((D, eb), lambda i, e: (0, e)),
            tspec, tspec, kspec, kspec,
            pl.BlockSpec((tm, D), lambda i, e: (i, 0), pipeline_mode=pl.Buffered(1)),
            pl.BlockSpec((1, D), lambda i, e: (0, 0)),
        ],
        out_specs=pl.BlockSpec((tm, D), lambda i, e: (i, 0)),
        scratch_shapes=[pltpu.VMEM((D, tm), F32), pltpu.VMEM((eb, tm), BF)],
        compiler_params=_cparams(("parallel", "arbitrary")),
        name="experts",
    )(hnt, u_b, v_t, r1, vh, n, uu, h, final_w)


def _band_matrices(tm):
    t = np.arange(tm)[:, None]
    j = np.arange(tm + 2 * POOL_HALO)[None, :]
    return np.stack([((j >= t + POOL_HALO - w // 2) & (j < t + POOL_HALO + w // 2)) for w in POOL_WINDOWS]
                    ).astype(np.float32)


def _tri_matrix():
    l = np.arange(CHUNK)[:, None]
    s = np.arange(CHUNK)[None, :]
    return np.concatenate([(s <= l), (s >= l)], axis=0).astype(np.float32)


def _dt_perm():
    cols = []
    for g in range(SSD_GROUPS):
        for d in range(2):
            for r in range(SSD_HPG):
                cols.append(d * SSD_HEADS + g * SSD_HPG + r)
    return np.asarray(cols)


def kernel(x, mixer_norm_w, w_in, conv_w, conv_b, dt_bias, a_log, d_skip, ssd_norm_w, w_ssd_branch, w_pool_group, pool_scale, w_pool_branch, w_out, ffn_norm_w, w_query, sub_keys, expert_u, expert_v, final_norm_w):
    return _forward(x, mixer_norm_w, w_in, conv_w, conv_b, dt_bias, a_log, d_skip, ssd_norm_w, w_ssd_branch,
                    w_pool_group, pool_scale, w_pool_branch, w_out, ffn_norm_w, w_query, sub_keys, expert_u,
                    expert_v, final_norm_w)[0]


def _forward(x, mixer_norm_w, w_in, conv_w, conv_b, dt_bias, a_log, d_skip, ssd_norm_w, w_ssd_branch, w_pool_group, pool_scale, w_pool_branch, w_out, ffn_norm_w, w_query, sub_keys, expert_u, expert_v, final_norm_w):
    B, S, D = x.shape
    T = B * S
    G, N, W = SSD_GROUPS, SSD_STATE, GROUP_W
    assert D == D_MODEL and mixer_norm_w.shape[0] == 1 and S % 512 == 0
    merge_tm = 256

    w_in0 = w_in[0]
    o1 = SSD_D_INNER
    o2 = o1 + SSD_XBC
    o3 = o2 + 2 * SSD_HEADS
    o4 = o3 + POOL_WIDTH
    w_main = jnp.concatenate([w_in0[:, :o2], w_in0[:, o4:], w_in0[:, o3:o4]], axis=1).astype(BF)
    perm = _dt_perm()
    pad = LANES - perm.size
    w_dt = jnp.pad(w_in0[:, o2:o3][:, perm], ((0, 0), (0, pad))).astype(BF)
    bias_row = jnp.pad(dt_bias[0].reshape(-1)[perm], (0, pad)).reshape(1, LANES)
    alog_row = jnp.pad(a_log[0].reshape(-1)[perm], (0, pad)).reshape(1, LANES)

    cw = conv_w[0, :, 0, :]
    cwx = cw[:, :SSD_D_INNER].reshape(SSD_CONV, G, W)
    cwb = cw[:, SSD_D_INNER:SSD_D_INNER + G * N].reshape(SSD_CONV, G, N)
    cwc = cw[:, SSD_D_INNER + G * N:].reshape(SSD_CONV, G, N)
    conv_w_g = jnp.pad(jnp.transpose(jnp.concatenate([cwx, cwb, cwc], axis=2), (1, 0, 2)),
                       ((0, 0), (0, 8 - SSD_CONV), (0, 0)))
    cbv = conv_b[0]
    conv_b_g = jnp.concatenate([cbv[:SSD_D_INNER].reshape(G, 1, W),
                                cbv[SSD_D_INNER:SSD_D_INNER + G * N].reshape(G, 1, N),
                                cbv[SSD_D_INNER + G * N:].reshape(G, 1, N)], axis=2)
    dskip_g = jnp.repeat(d_skip[0], SSD_HEAD_DIM).reshape(G, 1, W)
    normw_g = ssd_norm_w[0].reshape(G, 1, W)

    wq_t = w_query[0].T.astype(BF)
    keys = sub_keys[0].reshape(2 * PEER_HEADS, PEER_KEYS, PEER_HALF).astype(BF)
    u_b = expert_u[0].astype(BF)
    v_t = expert_v[0].T.astype(BF)

    x2 = x.reshape(T, D)
    proj, dt_raw = _in_proj(x2, mixer_norm_w, w_main, w_dt)
    dtcs, cst = _dt_prep(dt_raw.reshape(B, S, LANES), bias_row, alog_row, jnp.asarray(_tri_matrix(), BF))
    y_ssd = _ssd(proj.reshape(B, S, MAIN_W), dtcs, cst, conv_w_g, conv_b_g, dskip_g, normw_g)
    h, hnt = _merge(y_ssd.reshape(T, SSD_D_INNER), proj, x2,
                    w_ssd_branch[0].astype(BF), jnp.asarray(_band_matrices(merge_tm), BF),
                    w_pool_group[0].astype(BF), pool_scale, w_pool_branch[0].astype(BF),
                    w_out[0].astype(BF), ffn_norm_w, S, tm=merge_tm)
    r1, vh, n, uu = _route(hnt, wq_t, keys)
    out = _experts(hnt, u_b, v_t, r1, vh, n, uu, h, final_norm_w.reshape(1, D))
    return out.reshape(B, S, D), h
```

```python
import functools
import math

import numpy as np
import jax
import jax.numpy as jnp
from jax import lax
from jax.experimental import pallas as pl
from jax.experimental.pallas import tpu as pltpu

F32 = jnp.float32
BF = jnp.bfloat16

LANES = 128
V7X_VMEM_BYTES = 64 * 1024 * 1024
VMEM_LIMIT = V7X_VMEM_BYTES * 7 // 8

D_MODEL = 2048
SSD_HEAD_DIM = 64
SSD_D_INNER = D_MODEL
SSD_HEADS = SSD_D_INNER // SSD_HEAD_DIM
SSD_GROUPS = 8
SSD_HPG = SSD_HEADS // SSD_GROUPS
SSD_STATE = 128
SSD_CONV = 5
CHUNK = 128
SSD_XBC = SSD_D_INNER + 2 * SSD_GROUPS * SSD_STATE
GROUP_W = SSD_HPG * SSD_HEAD_DIM
POOL_WIDTH = D_MODEL // 2
POOL_WINDOWS = (2, 4, 8, 16)
POOL_GDIM = POOL_WIDTH // len(POOL_WINDOWS)
POOL_HALO = 64
CONV_HALO = 64
PEER_HEADS = 8
PEER_KEYS = 128
PEER_HALF = 128
PEER_TOPK = 16
EPS = 1e-6
NEG_BIG = -1e30

TM_INPROJ = 1024
TN_INPROJ = 1024
ROWS_DTPREP = 4 * CHUNK
TM_MERGE = 256
TM_ROUTE = 512
TM_EXPERTS = 512
KEYS_PER_STEP = 8

MAIN_W = SSD_D_INNER + SSD_XBC + 2 * D_MODEL + POOL_WIDTH
COL_XBC = SSD_D_INNER
COL_GATE = COL_XBC + SSD_XBC
COL_POOL = COL_GATE + 2 * D_MODEL


def _sigmoid(v):
    return 1.0 / (1.0 + jnp.exp(-v))


def _rms(v, w):
    return v * lax.rsqrt(jnp.mean(v * v, axis=-1, keepdims=True) + EPS) * w


def _cparams(sem):
    return pltpu.CompilerParams(dimension_semantics=sem, vmem_limit_bytes=VMEM_LIMIT)


def _resident(shape):
    nd = len(shape)
    return pl.BlockSpec(shape, lambda *_: (0,) * nd, pipeline_mode=pl.Buffered(1))


def _inproj_kernel(x_ref, nw_ref, w_ref, wdt_ref, o_ref, dt_ref, xn_ref):
    tm = x_ref.shape[0]

    @pl.when(pl.program_id(1) == 0)
    def _():
        def body(r, carry):
            rows = pl.ds(pl.multiple_of(r * CHUNK, CHUNK), CHUNK)
            xn = _rms(x_ref[rows, :], nw_ref[...]).astype(BF)
            xn_ref[rows, :] = xn
            dt_ref[rows, :] = jnp.dot(xn, wdt_ref[...], preferred_element_type=F32)
            return carry

        lax.fori_loop(0, tm // CHUNK, body, 0, unroll=2)

    o_ref[...] = jnp.dot(xn_ref[...], w_ref[...], preferred_element_type=F32).astype(BF)


def _in_proj(x2, norm_w, w_main, w_dt, tm=TM_INPROJ, tn=TN_INPROJ):
    T, D = x2.shape
    N = w_main.shape[1]
    return pl.pallas_call(
        _inproj_kernel,
        out_shape=(jax.ShapeDtypeStruct((T, N), BF), jax.ShapeDtypeStruct((T, LANES), F32)),
        grid=(T // tm, N // tn),
        in_specs=[
            pl.BlockSpec((tm, D), lambda i, j: (i, 0)),
            pl.BlockSpec((1, D), lambda i, j: (0, 0)),
            pl.BlockSpec((D, tn), lambda i, j: (0, j)),
            pl.BlockSpec((D, LANES), lambda i, j: (0, 0)),
        ],
        out_specs=(
            pl.BlockSpec((tm, tn), lambda i, j: (i, j)),
            pl.BlockSpec((tm, LANES), lambda i, j: (i, 0)),
        ),
        scratch_shapes=[pltpu.VMEM((tm, D), BF)],
        compiler_params=_cparams(("parallel", "arbitrary")),
        name="in_proj",
    )(x2, norm_w, w_main, w_dt)


def _split3(a):
    hi = a.astype(BF)
    r1 = a - hi.astype(F32)
    mid = r1.astype(BF)
    lo = (r1 - mid.astype(F32)).astype(BF)
    return hi, mid, lo


def _dtprep_kernel(raw_ref, bias_ref, alog_ref, tri_ref, col_ref, row_ref):
    nh = 2 * SSD_HPG
    lane = lax.broadcasted_iota(jnp.int32, (CHUNK, LANES), 1)
    backward = (lane % nh) >= SSD_HPG
    tri = tri_ref[...]
    for ci in range(raw_ref.shape[0] // CHUNK):
        rows = slice(ci * CHUNK, (ci + 1) * CHUNK)
        v = raw_ref[rows, :] + bias_ref[...]
        dt = jnp.maximum(v, 0.0) + jnp.log(1.0 + jnp.exp(-jnp.abs(v)))
        a = dt * (-jnp.exp(alog_ref[...]))
        cs2 = None
        for part in _split3(a):
            t = jnp.dot(tri, part, preferred_element_type=F32)
            cs2 = t if cs2 is None else cs2 + t
        cs = jnp.where(backward, cs2[CHUNK:, :], cs2[:CHUNK, :])
        edge = jnp.where(backward[0:1, :], cs[0:1, :], cs[CHUNK - 1:CHUNK, :])
        w = dt * jnp.exp(edge - cs)
        cs_t = cs.T
        dt_t = dt.T
        for g in range(SSD_GROUPS):
            wg = w if g == 0 else pltpu.roll(w, (LANES - nh * g) % LANES, axis=1)
            cg = pltpu.roll(cs, (LANES + nh - nh * g) % LANES, axis=1)
            col_ref[0, g, rows, :] = jnp.where(lane < nh, wg, jnp.where(lane < 2 * nh, cg, 0.0))
            row_ref[0, g, 0:nh, rows] = cs_t[nh * g:nh * (g + 1), :]
            row_ref[0, g, nh:2 * nh, rows] = dt_t[nh * g:nh * (g + 1), :]


def _dt_prep(dt_raw3, bias_row, alog_row, tri, rows=ROWS_DTPREP):
    B, S, _ = dt_raw3.shape
    nh = 2 * SSD_HPG
    return pl.pallas_call(
        _dtprep_kernel,
        out_shape=(jax.ShapeDtypeStruct((B, SSD_GROUPS, S, LANES), F32),
                   jax.ShapeDtypeStruct((B, SSD_GROUPS, 2 * nh, S), F32)),
        grid=(B, S // rows),
        in_specs=[
            pl.BlockSpec((None, rows, LANES), lambda b, c: (b, c, 0)),
            pl.BlockSpec((1, LANES), lambda b, c: (0, 0)),
            pl.BlockSpec((1, LANES), lambda b, c: (0, 0)),
            pl.BlockSpec((2 * CHUNK, CHUNK), lambda b, c: (0, 0)),
        ],
        out_specs=(
            pl.BlockSpec((1, SSD_GROUPS, rows, LANES), lambda b, c: (b, 0, c, 0)),
            pl.BlockSpec((1, SSD_GROUPS, 2 * nh, rows), lambda b, c: (b, 0, 0, c)),
        ),
        compiler_params=_cparams(("parallel", "parallel")),
        name="dt_prep",
    )(dt_raw3, bias_row, alog_row, tri)


def _ssd_kernel(xs_ref, b_ref, c_ref, z_ref, col_ref, row_ref, cw_ref, cb_ref, shift_ref, dsk_ref, nw_ref,
                y_ref, xlo_s, xhi_s, cm_s, bt_s, xdf_s, xdb_s, y_s, sf_s, sb_s, ef_s, eb_s, edge_s):
    S = xs_ref.shape[1]
    nc = S // CHUNK
    H = CONV_HALO
    N = SSD_STATE
    hpg = SSD_HPG

    li = lax.broadcasted_iota(jnp.int32, (CHUNK, CHUNK), 0)
    si = lax.broadcasted_iota(jnp.int32, (CHUNK, CHUNK), 1)
    lo_half = si < SSD_HEAD_DIM
    lo_half2 = jnp.concatenate([lo_half, lo_half], axis=1)

    def expand(bc):
        return jnp.concatenate([jnp.where(lo_half, bc[0], bc[1]), jnp.where(lo_half, bc[2], bc[3])], axis=1)

    def column(tile, k):
        return jnp.broadcast_to(tile[:, k:k + 1], (CHUNK, LANES))

    def halo_rows(src_ref, c):
        r0 = pl.multiple_of(c * CHUNK, CHUNK)
        cur = src_ref[0, pl.ds(r0, CHUNK), :]
        prev = src_ref[0, pl.ds(pl.multiple_of(jnp.maximum(r0 - H, 0), H), H), :]
        nxt = src_ref[0, pl.ds(pl.multiple_of(jnp.minimum(r0 + CHUNK, S - H), H), H), :]
        prev = jnp.where(c > 0, prev, jnp.zeros_like(prev))
        nxt = jnp.where(c < nc - 1, nxt, jnp.zeros_like(nxt))
        return jnp.concatenate([prev, cur, nxt], axis=0), cur

    def conv_body(c, carry):
        rows = pl.ds(pl.multiple_of(c * CHUNK, CHUNK), CHUNK)
        parts = [halo_rows(r, c) for r in (xs_ref, b_ref, c_ref)]
        cat = jnp.concatenate([p[0] for p in parts], axis=1)
        cur = jnp.concatenate([p[1] for p in parts], axis=1).astype(F32)
        sh = jnp.dot(shift_ref[...], cat, preferred_element_type=F32)
        w = cw_ref[0]
        acc = cb_ref[0] + w[2:3, :] * cur
        for j, k in enumerate((0, 1, 3, 4)):
            acc = acc + w[k:k + 1, :] * sh[j * CHUNK:(j + 1) * CHUNK, :]
        act = acc * _sigmoid(acc)
        xs = act[:, :GROUP_W]
        bm = act[:, GROUP_W:GROUP_W + N]
        xlo_s[rows, :] = jnp.where(lo_half2, xs, 0.0).astype(BF)
        xhi_s[rows, :] = jnp.where(lo_half2, 0.0, xs).astype(BF)
        cm_s[rows, :] = act[:, GROUP_W + N:].astype(BF)
        bt_s[:, rows] = bm.T.astype(BF)
        col = col_ref[0, 0, rows, :]
        xdf_s[rows, :] = (xs * expand([column(col, r) for r in range(hpg)])).astype(BF)
        xdb_s[rows, :] = (xs * expand([column(col, hpg + r) for r in range(hpg)])).astype(BF)
        return carry

    lax.fori_loop(0, nc, conv_body, 0, unroll=4)

    def diag_body(c, carry):
        rows = pl.ds(pl.multiple_of(c * CHUNK, CHUNK), CHUNK)
        bm_t = bt_s[:, rows]
        cm = cm_s[rows, :]
        col = col_ref[0, 0, rows, :]
        row = row_ref[0, 0, :, rows]
        cb = jnp.dot(cm, bm_t, preferred_element_type=F32)
        csb_f = [column(col, 2 * hpg + r) for r in range(hpg)]
        csb_b = [column(col, 3 * hpg + r) for r in range(hpg)]
        ms = []
        for r in range(hpg):
            wf = (jnp.exp(jnp.where(li >= si, csb_f[r] - row[r:r + 1, :], NEG_BIG))
                  * row[2 * hpg + r:2 * hpg + r + 1, :])
            wb = (jnp.exp(jnp.where(li <= si, csb_b[r] - row[hpg + r:hpg + r + 1, :], NEG_BIG))
                  * row[3 * hpg + r:3 * hpg + r + 1, :])
            ms.append((cb * (wf + wb)).astype(BF))
        ys = []
        for p in range(2):
            lanes = slice(LANES * p, LANES * (p + 1))
            m2 = jnp.concatenate([ms[2 * p], ms[2 * p + 1]], axis=1)
            x2 = jnp.concatenate([xlo_s[rows, lanes], xhi_s[rows, lanes]], axis=0)
            ys.append(jnp.dot(m2, x2, preferred_element_type=F32))
        y_s[rows, :] = jnp.concatenate(ys, axis=1)
        sf_s[c] = jnp.dot(bm_t, xdf_s[rows, :], preferred_element_type=F32)
        sb_s[c] = jnp.dot(bm_t, xdb_s[rows, :], preferred_element_type=F32)
        ecs_f = jnp.exp(expand(csb_f))
        ecs_b = jnp.exp(expand(csb_b))
        ef_s[rows, :] = ecs_f.astype(BF)
        eb_s[rows, :] = ecs_b.astype(BF)
        edge_s[c, 0:1, :] = ecs_f[CHUNK - 1:CHUNK, :]
        edge_s[c, 1:2, :] = ecs_b[0:1, :]
        return carry

    lax.fori_loop(0, nc, diag_body, 0, unroll=4)

    def state_body(i, carry):
        hf, hb = carry
        j = nc - 1 - i
        s_f = sf_s[i]
        s_b = sb_s[j]
        sf_s[i] = hf
        sb_s[j] = hb
        return hf * edge_s[i, 0:1, :] + s_f, hb * edge_s[j, 1:2, :] + s_b

    zero_state = jnp.zeros((N, GROUP_W), F32)
    lax.fori_loop(0, nc, state_body, (zero_state, zero_state))

    def out_body(c, carry):
        rows = pl.ds(pl.multiple_of(c * CHUNK, CHUNK), CHUNK)
        cm = cm_s[rows, :]
        yoff = (jnp.dot(cm, sf_s[c].astype(BF), preferred_element_type=F32) * ef_s[rows, :].astype(F32)
                + jnp.dot(cm, sb_s[c].astype(BF), preferred_element_type=F32) * eb_s[rows, :].astype(F32))
        xs = (xlo_s[rows, :] + xhi_s[rows, :]).astype(F32)
        y = y_s[rows, :] + yoff + dsk_ref[0] * xs
        z = z_ref[0, rows, :].astype(F32)
        y = y * (z * _sigmoid(z))
        y_ref[0, rows, :] = _rms(y, nw_ref[0]).astype(BF)
        return carry

    lax.fori_loop(0, nc, out_body, 0, unroll=4)


def _shift_matrices():
    t = np.arange(CHUNK)[:, None]
    j = np.arange(CHUNK + 2 * CONV_HALO)[None, :]
    return np.concatenate([(j == t + CONV_HALO + k - 2) for k in (0, 1, 3, 4)], axis=0).astype(np.float32)


def _ssd(proj3, col_tbl, row_tbl, conv_w_g, conv_b_g, dskip_g, normw_g):
    B, S, _ = proj3.shape
    G, N, W = SSD_GROUPS, SSD_STATE, GROUP_W
    xs_blk = COL_XBC // W
    b_blk = (COL_XBC + SSD_D_INNER) // N
    c_blk = b_blk + G
    shift = jnp.asarray(_shift_matrices(), BF)
    return pl.pallas_call(
        _ssd_kernel,
        out_shape=jax.ShapeDtypeStruct((B, S, SSD_D_INNER), BF),
        grid=(B, G),
        in_specs=[
            pl.BlockSpec((1, S, W), lambda b, g: (b, 0, xs_blk + g)),
            pl.BlockSpec((1, S, N), lambda b, g: (b, 0, b_blk + g)),
            pl.BlockSpec((1, S, N), lambda b, g: (b, 0, c_blk + g)),
            pl.BlockSpec((1, S, W), lambda b, g: (b, 0, g)),
            pl.BlockSpec((1, 1, S, LANES), lambda b, g: (b, g, 0, 0)),
            pl.BlockSpec((1, 1, 4 * SSD_HPG, S), lambda b, g: (b, g, 0, 0)),
            pl.BlockSpec((1, 8, W + 2 * N), lambda b, g: (g, 0, 0)),
            pl.BlockSpec((1, 1, W + 2 * N), lambda b, g: (g, 0, 0)),
            pl.BlockSpec(shift.shape, lambda b, g: (0, 0)),
            pl.BlockSpec((1, 1, W), lambda b, g: (g, 0, 0)),
            pl.BlockSpec((1, 1, W), lambda b, g: (g, 0, 0)),
        ],
        out_specs=pl.BlockSpec((1, S, W), lambda b, g: (b, 0, g)),
        scratch_shapes=[
            pltpu.VMEM((S, W), BF),
            pltpu.VMEM((S, W), BF),
            pltpu.VMEM((S, N), BF),
            pltpu.VMEM((N, S), BF),
            pltpu.VMEM((S, W), BF),
            pltpu.VMEM((S, W), BF),
            pltpu.VMEM((S, W), F32),
            pltpu.VMEM((S // CHUNK, N, W), F32),
            pltpu.VMEM((S // CHUNK, N, W), F32),
            pltpu.VMEM((S, W), BF),
            pltpu.VMEM((S, W), BF),
            pltpu.VMEM((S // CHUNK, 8, W), F32),
        ],
        compiler_params=_cparams(("parallel", "parallel")),
        name="ssd",
    )(proj3, proj3, proj3, proj3, col_tbl, row_tbl, conv_w_g, conv_b_g, shift, dskip_g, normw_g)


def _merge_kernel(seq_len, ys_ref, xp_ref, xprev_ref, xnext_ref, gate1_ref, gate2_ref, x_ref,
                  wssd_ref, band_ref, wg_ref, pscale_ref, wpool_ref, wout_ref, fnw_ref,
                  h_ref, hnt_ref):
    tm = ys_ref.shape[0]
    i = pl.program_id(0)
    pos0 = (i * tm) % seq_len
    yssd = jnp.dot(ys_ref[...], wssd_ref[...], preferred_element_type=F32)

    cur = xp_ref[...]
    zero = jnp.zeros((POOL_HALO, POOL_WIDTH), BF)
    prev = jnp.where(pos0 > 0, xprev_ref[...], zero)
    nxt = jnp.where(pos0 + tm < seq_len, xnext_ref[...], zero)
    cat = jnp.concatenate([prev, cur, nxt], axis=0)
    pos = pos0 + lax.broadcasted_iota(jnp.int32, (tm, 1), 0)
    pooled = []
    for gi, w in enumerate(POOL_WINDOWS):
        cols = slice(gi * POOL_GDIM, (gi + 1) * POOL_GDIM)
        wsum = jnp.dot(band_ref[gi], cat[:, cols], preferred_element_type=F32)
        cnt = jnp.minimum(pos + w // 2, seq_len) - jnp.maximum(pos - w // 2, 0)
        pg = wsum / cnt.astype(F32) - cur[:, cols].astype(F32)
        pg = jnp.dot(pg.astype(BF), wg_ref[gi], preferred_element_type=F32)
        pooled.append((pg * pscale_ref[:, cols]).astype(BF))
    ypool = jnp.dot(jnp.concatenate(pooled, axis=1), wpool_ref[...], preferred_element_type=F32)

    merged = _sigmoid(gate1_ref[...].astype(F32)) * yssd + _sigmoid(gate2_ref[...].astype(F32)) * ypool
    h = x_ref[...] + jnp.dot(merged.astype(BF), wout_ref[...], preferred_element_type=F32)
    h_ref[...] = h
    hnt_ref[...] = _rms(h, fnw_ref[...]).T.astype(BF)


def _merge(y_ssd, proj, x2, w_ssd, band, w_g, pool_scale, w_pool, w_out, ffn_w, seq_len, tm=TM_MERGE):
    T, D = x2.shape
    hb = tm // POOL_HALO
    last_hb = T // POOL_HALO - 1
    pool_blk = COL_POOL // POOL_WIDTH
    g1_blk = COL_GATE // D
    return pl.pallas_call(
        functools.partial(_merge_kernel, seq_len),
        out_shape=(jax.ShapeDtypeStruct((T, D), F32), jax.ShapeDtypeStruct((D, T), BF)),
        grid=(T // tm,),
        in_specs=[
            pl.BlockSpec((tm, D), lambda i: (i, 0)),
            pl.BlockSpec((tm, POOL_WIDTH), lambda i: (i, pool_blk)),
            pl.BlockSpec((POOL_HALO, POOL_WIDTH), lambda i: (jnp.maximum(i * hb - 1, 0), pool_blk)),
            pl.BlockSpec((POOL_HALO, POOL_WIDTH), lambda i: (jnp.minimum((i + 1) * hb, last_hb), pool_blk)),
            pl.BlockSpec((tm, D), lambda i: (i, g1_blk)),
            pl.BlockSpec((tm, D), lambda i: (i, g1_blk + 1)),
            pl.BlockSpec((tm, D), lambda i: (i, 0)),
            _resident(w_ssd.shape),
            _resident(band.shape),
            _resident(w_g.shape),
            _resident(pool_scale.shape),
            _resident(w_pool.shape),
            _resident(w_out.shape),
            _resident(ffn_w.shape),
        ],
        out_specs=(
            pl.BlockSpec((tm, D), lambda i: (i, 0)),
            pl.BlockSpec((D, tm), lambda i: (0, i)),
        ),
        compiler_params=_cparams(("parallel",)),
        name="merge",
    )(y_ssd, proj, proj, proj, proj, proj, x2, w_ssd, band, w_g, pool_scale, w_pool, w_out, ffn_w)


def _cmpx(v, i, j):
    hi = jnp.maximum(v[i], v[j])
    lo = jnp.minimum(v[i], v[j])
    v[i], v[j] = hi, lo


def _bitonic_merge16(v):
    v = list(v)
    for j in (8, 4, 2, 1):
        for i in range(16):
            if i ^ j > i:
                _cmpx(v, i, i ^ j)
    return v


def _bitonic_sort16(v):
    v = list(v)
    for k in (2, 4, 8):
        for j in [s for s in (4, 2, 1) if s < k]:
            for i in range(16):
                l = i ^ j
                if l > i:
                    if i & k == 0:
                        _cmpx(v, i, l)
                    else:
                        _cmpx(v, l, i)
    return _bitonic_merge16(v)


def _top16_union(p, q):
    return _bitonic_merge16([jnp.maximum(p[i], q[15 - i]) for i in range(16)])


def _prefix_count(bs, pred):
    m8 = pred(bs[7])
    m4 = pred(jnp.where(m8, bs[11], bs[3]))
    m2 = pred(jnp.where(m8, jnp.where(m4, bs[13], bs[9]), jnp.where(m4, bs[5], bs[1])))
    lo = jnp.where(m4, jnp.where(m2, bs[6], bs[4]), jnp.where(m2, bs[2], bs[0]))
    hi = jnp.where(m4, jnp.where(m2, bs[14], bs[12]), jnp.where(m2, bs[10], bs[8]))
    m1 = pred(jnp.where(m8, hi, lo))
    cnt = (jnp.where(m8, 8.0, 0.0) + jnp.where(m4, 4.0, 0.0)) + (jnp.where(m2, 2.0, 0.0) + jnp.where(m1, 1.0, 0.0))
    return jnp.where(pred(bs[15]), 16.0, cnt)


def _route_kernel(hnt_ref, wq_ref, keys_ref, r1_ref, vh_ref, n_ref, u_ref, q_s, sc_s, ac_s, bc_s, tz_s):
    tm = hnt_ref.shape[1]
    nlb = tm // LANES
    K = PEER_TOPK
    qrows = 4 * PEER_HALF

    for r0 in range(0, q_s.shape[0], qrows):
        q_s[r0:r0 + qrows, :] = jnp.dot(wq_ref[r0:r0 + qrows, :], hnt_ref[...],
                                        preferred_element_type=F32).astype(BF)
    for hk in range(2 * PEER_HEADS):
        sc_s[hk] = jnp.dot(keys_ref[hk], q_s[hk * PEER_HALF:(hk + 1) * PEER_HALF, :],
                           preferred_element_type=F32)
    ac_s[...] = jnp.zeros_like(ac_s)
    bc_s[...] = jnp.zeros_like(bc_s)

    sub = lax.broadcasted_iota(jnp.int32, (8, LANES), 0)

    def top_body(idx, carry):
        h = idx // nlb
        lanes = pl.ds(pl.multiple_of((idx % nlb) * LANES, LANES), LANES)
        for half, dst in ((0, ac_s), (1, bc_s)):
            v = _bitonic_sort16([sc_s[2 * h + half, pl.ds(8 * k, 8), lanes] for k in range(PEER_KEYS // 8)])
            for sh in (4, 2, 1):
                v = _bitonic_merge16([jnp.maximum(v[i], pltpu.roll(v[15 - i], sh, axis=0)) for i in range(16)])
            for r in range(K):
                dst[r, :, lanes] = jnp.where(sub == h, v[r], dst[r, :, lanes])
        return carry

    lax.fori_loop(0, PEER_HEADS * nlb, top_body, 0)

    def pair_body(lb, carry):
        lanes = pl.ds(pl.multiple_of(lb * LANES, LANES), LANES)
        a = [ac_s[r, :, lanes] for r in range(K)]
        b = [bc_s[r, :, lanes] for r in range(K)]
        first = [a[0] + b[j] for j in range(K)]
        rest = [a[i] + b[j] for i in range(1, K) for j in range(K // (i + 1))]
        rest = rest + [jnp.full_like(a[0], -jnp.inf)] * (3 * K - len(rest))
        groups = [_bitonic_sort16(rest[K * g:K * (g + 1)]) for g in range(3)]
        top = _top16_union(_top16_union(first, groups[0]), _top16_union(groups[1], groups[2]))
        z = None
        for r in range(K):
            e = jnp.exp(top[r] - top[0])
            z = e if z is None else z + e
        tz_s[0, :, lanes] = top[K - 1]
        tz_s[1, :, lanes] = 1.0 / z
        return carry

    lax.fori_loop(0, nlb, pair_body, 0)

    def table_body(lb, carry):
        lanes = pl.ds(pl.multiple_of(lb * LANES, LANES), LANES)
        for h in range(PEER_HEADS):
            s0 = sc_s[2 * h, :, lanes]
            s1 = sc_s[2 * h + 1, :, lanes]
            tau = tz_s[0, h:h + 1, lanes]
            bs = [bc_s[r, h:h + 1, lanes] for r in range(K)]
            r1_ref[h, :, lanes] = (_prefix_count(bs, lambda b: b > s1) + 1.0).astype(BF)
            n_ref[h, :, lanes] = _prefix_count(bs, lambda b: s0 + b >= tau)
            vh_ref[h, :, lanes] = jnp.exp(s1 - bc_s[0, h:h + 1, lanes]).astype(BF)
            u_ref[h, :, lanes] = jnp.exp(s0 - ac_s[0, h:h + 1, lanes]) * tz_s[1, h:h + 1, lanes]
        return carry

    lax.fori_loop(0, nlb, table_body, 0)


def _route(hnt, wq_t, keys, tm=TM_ROUTE):
    D, T = hnt.shape
    H, NK = PEER_HEADS, PEER_KEYS
    tbl = lambda dt: jax.ShapeDtypeStruct((H, NK, T), dt)
    tspec = pl.BlockSpec((H, NK, tm), lambda i: (0, 0, i))
    return pl.pallas_call(
        _route_kernel,
        out_shape=(tbl(BF), tbl(BF), tbl(F32), tbl(F32)),
        grid=(T // tm,),
        in_specs=[
            pl.BlockSpec((D, tm), lambda i: (0, i)),
            _resident(wq_t.shape),
            _resident(keys.shape),
        ],
        out_specs=(tspec, tspec, tspec, tspec),
        scratch_shapes=[
            pltpu.VMEM((wq_t.shape[0], tm), BF),
            pltpu.VMEM((2 * H, NK, tm), F32),
            pltpu.VMEM((PEER_TOPK, H, tm), F32),
            pltpu.VMEM((PEER_TOPK, H, tm), F32),
            pltpu.VMEM((2, H, tm), F32),
        ],
        compiler_params=_cparams(("parallel",)),
        name="route",
    )(hnt, wq_t, keys)


def _experts_kernel(hnt_ref, u_ref, vt_ref, r1_ref, vh_ref, n_ref, uu_ref, h_ref, fw_ref,
                    o_ref, acc_s, coef_s):
    eb = u_ref.shape[0]
    tm = hnt_ref.shape[1]
    e = pl.program_id(1)

    @pl.when(e == 0)
    def _():
        acc_s[...] = jnp.zeros_like(acc_s)

    a_t = jnp.dot(u_ref[...], hnt_ref[...], preferred_element_type=F32)
    for ib in range(eb // PEER_KEYS):
        a = a_t[ib * PEER_KEYS:(ib + 1) * PEER_KEYS, :]
        ab = a.astype(BF)
        act = (0.5 * ab) * (1.0 + lax.erf(ab * math.sqrt(0.5)))
        gate = jnp.zeros((PEER_KEYS, tm), BF)
        for h in range(PEER_HEADS):
            nrow = jnp.broadcast_to(n_ref[h, ib:ib + 1, :], (PEER_KEYS, tm)).astype(BF)
            urow = jnp.broadcast_to(uu_ref[h, ib:ib + 1, :], (PEER_KEYS, tm)).astype(BF)
            sel = jnp.where(r1_ref[h] <= nrow, vh_ref[h], jnp.zeros((PEER_KEYS, tm), BF))
            gate = gate + sel * urow
        coef_s[ib * PEER_KEYS:(ib + 1) * PEER_KEYS, :] = act * gate
    acc_s[...] += jnp.dot(vt_ref[...], coef_s[...], preferred_element_type=F32)

    @pl.when(e == pl.num_programs(1) - 1)
    def _():
        o_ref[...] = _rms(h_ref[...] + acc_s[...].T, fw_ref[...])


def _experts(hnt, u_b, v_t, r1, vh, n, uu, h, final_w, tm=TM_EXPERTS):
    D, T = hnt.shape
    E = u_b.shape[0]
    H, NK = PEER_HEADS, PEER_KEYS
    kb = KEYS_PER_STEP
    eb = kb * NK
    tspec = pl.BlockSpec((H, NK, tm), lambda i, e: (0, 0, i), pipeline_mode=pl.Buffered(1))
    kspec = pl.BlockSpec((H, kb, tm), lambda i, e: (0, e, i))
    return pl.pallas_call(
        _experts_kernel,
        out_shape=jax.ShapeDtypeStruct((T, D), F32),
        grid=(T // tm, E // eb),
        in_specs=[
            pl.BlockSpec((D, tm), lambda i, e: (0, i), pipeline_mode=pl.Buffered(1)),
            pl.BlockSpec((eb, D), lambda i, e: (e, 0)),
            pl.BlockSpec((D, eb), lambda i, e: (0, e)),
            tspec, tspec, kspec, kspec,
            pl.BlockSpec((tm, D), lambda i, e: (i, 0), pipeline_mode=pl.Buffered(1)),
            pl.BlockSpec((1, D), lambda i, e: (0, 0)),
        ],
        out_specs=pl.BlockSpec((tm, D), lambda i, e: (i, 0)),
        scratch_shapes=[pltpu.VMEM((D, tm), F32), pltpu.VMEM((eb, tm), BF)],
        compiler_params=_cparams(("parallel", "arbitrary")),
        name="experts",
    )(hnt, u_b, v_t, r1, vh, n, uu, h, final_w)


def _band_matrices(tm):
    t = np.arange(tm)[:, None]
    j = np.arange(tm + 2 * POOL_HALO)[None, :]
    return np.stack([((j >= t + POOL_HALO - w // 2) & (j < t + POOL_HALO + w // 2)) for w in POOL_WINDOWS]
                    ).astype(np.float32)


def _tri_matrix():
    l = np.arange(CHUNK)[:, None]
    s = np.arange(CHUNK)[None, :]
    return np.concatenate([(s <= l), (s >= l)], axis=0).astype(np.float32)


def _dt_perm():
    cols = []
    for g in range(SSD_GROUPS):
        for d in range(2):
            for r in range(SSD_HPG):
                cols.append(d * SSD_HEADS + g * SSD_HPG + r)
    return np.asarray(cols)


def kernel(x, mixer_norm_w, w_in, conv_w, conv_b, dt_bias, a_log, d_skip, ssd_norm_w, w_ssd_branch, w_pool_group, pool_scale, w_pool_branch, w_out, ffn_norm_w, w_query, sub_keys, expert_u, expert_v, final_norm_w):
    B, S, D = x.shape
    T = B * S
    G, N, W = SSD_GROUPS, SSD_STATE, GROUP_W
    assert D == D_MODEL and mixer_norm_w.shape[0] == 1, "one layer of width D_MODEL"
    assert S % ROWS_DTPREP == 0 and S % TM_MERGE == 0, "dt_prep / merge tiles must not straddle sequences"
    assert all(T % t == 0 for t in (TM_INPROJ, TM_ROUTE, TM_EXPERTS))

    w_in0 = w_in[0]
    o1 = SSD_D_INNER
    o2 = o1 + SSD_XBC
    o3 = o2 + 2 * SSD_HEADS
    o4 = o3 + POOL_WIDTH
    w_main = jnp.concatenate([w_in0[:, :o2], w_in0[:, o4:], w_in0[:, o3:o4]], axis=1).astype(BF)
    perm = _dt_perm()
    pad = LANES - perm.size
    w_dt = jnp.pad(w_in0[:, o2:o3][:, perm], ((0, 0), (0, pad))).astype(BF)
    bias_row = jnp.pad(dt_bias[0].reshape(-1)[perm], (0, pad)).reshape(1, LANES)
    alog_row = jnp.pad(a_log[0].reshape(-1)[perm], (0, pad)).reshape(1, LANES)

    cw = conv_w[0, :, 0, :]
    cwx = cw[:, :SSD_D_INNER].reshape(SSD_CONV, G, W)
    cwb = cw[:, SSD_D_INNER:SSD_D_INNER + G * N].reshape(SSD_CONV, G, N)
    cwc = cw[:, SSD_D_INNER + G * N:].reshape(SSD_CONV, G, N)
    conv_w_g = jnp.pad(jnp.transpose(jnp.concatenate([cwx, cwb, cwc], axis=2), (1, 0, 2)),
                       ((0, 0), (0, 8 - SSD_CONV), (0, 0)))
    cbv = conv_b[0]
    conv_b_g = jnp.concatenate([cbv[:SSD_D_INNER].reshape(G, 1, W),
                                cbv[SSD_D_INNER:SSD_D_INNER + G * N].reshape(G, 1, N),
                                cbv[SSD_D_INNER + G * N:].reshape(G, 1, N)], axis=2)
    dskip_g = jnp.repeat(d_skip[0], SSD_HEAD_DIM).reshape(G, 1, W)
    normw_g = ssd_norm_w[0].reshape(G, 1, W)

    wq_t = w_query[0].T.astype(BF)
    keys = sub_keys[0].reshape(2 * PEER_HEADS, PEER_KEYS, PEER_HALF).astype(BF)
    u_b = expert_u[0].astype(BF)
    v_t = expert_v[0].T.astype(BF)

    x2 = x.reshape(T, D)
    proj, dt_raw = _in_proj(x2, mixer_norm_w, w_main, w_dt)
    dtcs, cst = _dt_prep(dt_raw.reshape(B, S, LANES), bias_row, alog_row, jnp.asarray(_tri_matrix(), BF))
    y_ssd = _ssd(proj.reshape(B, S, MAIN_W), dtcs, cst, conv_w_g, conv_b_g, dskip_g, normw_g)
    h, hnt = _merge(y_ssd.reshape(T, SSD_D_INNER), proj, x2,
                    w_ssd_branch[0].astype(BF), jnp.asarray(_band_matrices(TM_MERGE), BF),
                    w_pool_group[0].astype(BF), pool_scale, w_pool_branch[0].astype(BF),
                    w_out[0].astype(BF), ffn_norm_w, S)
    r1, vh, n, uu = _route(hnt, wq_t, keys)
    out = _experts(hnt, u_b, v_t, r1, vh, n, uu, h, final_norm_w.reshape(1, D))
    return out.reshape(B, S, D)
```

```python
import functools
import math

import numpy as np
import jax
import jax.numpy as jnp
from jax import lax
from jax.experimental import pallas as pl
from jax.experimental.pallas import tpu as pltpu

F32 = jnp.float32
BF = jnp.bfloat16

LANES = 128
V7X_VMEM_BYTES = 64 * 1024 * 1024
VMEM_LIMIT = V7X_VMEM_BYTES * 7 // 8

D_MODEL = 2048
SSD_HEAD_DIM = 64
SSD_D_INNER = D_MODEL
SSD_HEADS = SSD_D_INNER // SSD_HEAD_DIM
SSD_GROUPS = 8
SSD_HPG = SSD_HEADS // SSD_GROUPS
SSD_STATE = 128
SSD_CONV = 5
CHUNK = 128
SSD_XBC = SSD_D_INNER + 2 * SSD_GROUPS * SSD_STATE
GROUP_W = SSD_HPG * SSD_HEAD_DIM
POOL_WIDTH = D_MODEL // 2
POOL_WINDOWS = (2, 4, 8, 16)
POOL_GDIM = POOL_WIDTH // len(POOL_WINDOWS)
POOL_HALO = 64
CONV_HALO = 64
PEER_HEADS = 8
PEER_KEYS = 128
PEER_HALF = 128
PEER_TOPK = 16
EPS = 1e-6
NEG_BIG = -1e30

TM_INPROJ = 1024
TN_INPROJ = 1024
ROWS_DTPREP = 4 * CHUNK
TM_MERGE = 256
TM_ROUTE = 512
TM_EXPERTS = 512
KEYS_PER_STEP = 8

MAIN_W = SSD_D_INNER + SSD_XBC + 2 * D_MODEL + POOL_WIDTH
COL_XBC = SSD_D_INNER
COL_GATE = COL_XBC + SSD_XBC
COL_POOL = COL_GATE + 2 * D_MODEL


def _sigmoid(v):
    return 1.0 / (1.0 + jnp.exp(-v))


def _rms(v, w):
    return v * lax.rsqrt(jnp.mean(v * v, axis=-1, keepdims=True) + EPS) * w


def _cparams(sem):
    return pltpu.CompilerParams(dimension_semantics=sem, vmem_limit_bytes=VMEM_LIMIT)


def _resident(shape):
    nd = len(shape)
    return pl.BlockSpec(shape, lambda *_: (0,) * nd, pipeline_mode=pl.Buffered(1))


def _inproj_kernel(x_ref, nw_ref, w_ref, wdt_ref, o_ref, dt_ref, xn_ref):
    tm = x_ref.shape[0]

    @pl.when(pl.program_id(1) == 0)
    def _():
        def body(r, carry):
            rows = pl.ds(pl.multiple_of(r * CHUNK, CHUNK), CHUNK)
            xn = _rms(x_ref[rows, :], nw_ref[...]).astype(BF)
            xn_ref[rows, :] = xn
            dt_ref[rows, :] = jnp.dot(xn, wdt_ref[...], preferred_element_type=F32)
            return carry

        lax.fori_loop(0, tm // CHUNK, body, 0, unroll=2)

    o_ref[...] = jnp.dot(xn_ref[...], w_ref[...], preferred_element_type=F32).astype(BF)


def _in_proj(x2, norm_w, w_main, w_dt, tm=TM_INPROJ, tn=TN_INPROJ):
    T, D = x2.shape
    N = w_main.shape[1]
    return pl.pallas_call(
        _inproj_kernel,
        out_shape=(jax.ShapeDtypeStruct((T, N), BF), jax.ShapeDtypeStruct((T, LANES), F32)),
        grid=(T // tm, N // tn),
        in_specs=[
            pl.BlockSpec((tm, D), lambda i, j: (i, 0)),
            pl.BlockSpec((1, D), lambda i, j: (0, 0)),
            pl.BlockSpec((D, tn), lambda i, j: (0, j)),
            pl.BlockSpec((D, LANES), lambda i, j: (0, 0)),
        ],
        out_specs=(
            pl.BlockSpec((tm, tn), lambda i, j: (i, j)),
            pl.BlockSpec((tm, LANES), lambda i, j: (i, 0)),
        ),
        scratch_shapes=[pltpu.VMEM((tm, D), BF)],
        compiler_params=_cparams(("parallel", "arbitrary")),
        name="in_proj",
    )(x2, norm_w, w_main, w_dt)


def _split3(a):
    hi = a.astype(BF)
    r1 = a - hi.astype(F32)
    mid = r1.astype(BF)
    lo = (r1 - mid.astype(F32)).astype(BF)
    return hi, mid, lo


def _dtprep_kernel(raw_ref, bias_ref, alog_ref, tri_ref, col_ref, row_ref):
    nh = 2 * SSD_HPG
    lane = lax.broadcasted_iota(jnp.int32, (CHUNK, LANES), 1)
    backward = (lane % nh) >= SSD_HPG
    tri = tri_ref[...]
    for ci in range(raw_ref.shape[0] // CHUNK):
        rows = slice(ci * CHUNK, (ci + 1) * CHUNK)
        v = raw_ref[rows, :] + bias_ref[...]
        dt = jnp.maximum(v, 0.0) + jnp.log(1.0 + jnp.exp(-jnp.abs(v)))
        a = dt * (-jnp.exp(alog_ref[...]))
        cs2 = None
        for part in _split3(a):
            t = jnp.dot(tri, part, preferred_element_type=F32)
            cs2 = t if cs2 is None else cs2 + t
        cs = jnp.where(backward, cs2[CHUNK:, :], cs2[:CHUNK, :])
        edge = jnp.where(backward[0:1, :], cs[0:1, :], cs[CHUNK - 1:CHUNK, :])
        w = dt * jnp.exp(edge - cs)
        cs_t = cs.T
        dt_t = dt.T
        for g in range(SSD_GROUPS):
            wg = w if g == 0 else pltpu.roll(w, (LANES - nh * g) % LANES, axis=1)
            cg = pltpu.roll(cs, (LANES + nh - nh * g) % LANES, axis=1)
            col_ref[0, g, rows, :] = jnp.where(lane < nh, wg, jnp.where(lane < 2 * nh, cg, 0.0))
            row_ref[0, g, 0:nh, rows] = cs_t[nh * g:nh * (g + 1), :]
            row_ref[0, g, nh:2 * nh, rows] = dt_t[nh * g:nh * (g + 1), :]


def _dt_prep(dt_raw3, bias_row, alog_row, tri, rows=ROWS_DTPREP):
    B, S, _ = dt_raw3.shape
    nh = 2 * SSD_HPG
    return pl.pallas_call(
        _dtprep_kernel,
        out_shape=(jax.ShapeDtypeStruct((B, SSD_GROUPS, S, LANES), F32),
                   jax.ShapeDtypeStruct((B, SSD_GROUPS, 2 * nh, S), F32)),
        grid=(B, S // rows),
        in_specs=[
            pl.BlockSpec((None, rows, LANES), lambda b, c: (b, c, 0)),
            pl.BlockSpec((1, LANES), lambda b, c: (0, 0)),
            pl.BlockSpec((1, LANES), lambda b, c: (0, 0)),
            pl.BlockSpec((2 * CHUNK, CHUNK), lambda b, c: (0, 0)),
        ],
        out_specs=(
            pl.BlockSpec((1, SSD_GROUPS, rows, LANES), lambda b, c: (b, 0, c, 0)),
            pl.BlockSpec((1, SSD_GROUPS, 2 * nh, rows), lambda b, c: (b, 0, 0, c)),
        ),
        compiler_params=_cparams(("parallel", "parallel")),
        name="dt_prep",
    )(dt_raw3, bias_row, alog_row, tri)


def _ssd_kernel(xs_ref, b_ref, c_ref, z_ref, col_ref, row_ref, cw_ref, cb_ref, shift_ref, dsk_ref, nw_ref,
                y_ref, xlo_s, xhi_s, cm_s, bt_s, xdf_s, xdb_s, y_s, sf_s, sb_s, ef_s, eb_s, edge_s):
    S = xs_ref.shape[1]
    nc = S // CHUNK
    H = CONV_HALO
    N = SSD_STATE
    hpg = SSD_HPG

    li = lax.broadcasted_iota(jnp.int32, (CHUNK, CHUNK), 0)
    si = lax.broadcasted_iota(jnp.int32, (CHUNK, CHUNK), 1)
    lo_half = si < SSD_HEAD_DIM
    lo_half2 = jnp.concatenate([lo_half, lo_half], axis=1)

    def expand(bc):
        return jnp.concatenate([jnp.where(lo_half, bc[0], bc[1]), jnp.where(lo_half, bc[2], bc[3])], axis=1)

    def column(tile, k):
        return jnp.broadcast_to(tile[:, k:k + 1], (CHUNK, LANES))

    def halo_rows(src_ref, c):
        r0 = pl.multiple_of(c * CHUNK, CHUNK)
        cur = src_ref[0, pl.ds(r0, CHUNK), :]
        prev = src_ref[0, pl.ds(pl.multiple_of(jnp.maximum(r0 - H, 0), H), H), :]
        nxt = src_ref[0, pl.ds(pl.multiple_of(jnp.minimum(r0 + CHUNK, S - H), H), H), :]
        prev = jnp.where(c > 0, prev, jnp.zeros_like(prev))
        nxt = jnp.where(c < nc - 1, nxt, jnp.zeros_like(nxt))
        return jnp.concatenate([prev, cur, nxt], axis=0), cur

    def conv_body(c, carry):
        rows = pl.ds(pl.multiple_of(c * CHUNK, CHUNK), CHUNK)
        parts = [halo_rows(r, c) for r in (xs_ref, b_ref, c_ref)]
        cat = jnp.concatenate([p[0] for p in parts], axis=1)
        cur = jnp.concatenate([p[1] for p in parts], axis=1).astype(F32)
        sh = jnp.dot(shift_ref[...], cat, preferred_element_type=F32)
        w = cw_ref[0]
        acc = cb_ref[0] + w[2:3, :] * cur
        for j, k in enumerate((0, 1, 3, 4)):
            acc = acc + w[k:k + 1, :] * sh[j * CHUNK:(j + 1) * CHUNK, :]
        act = acc * _sigmoid(acc)
        xs = act[:, :GROUP_W]
        bm = act[:, GROUP_W:GROUP_W + N]
        xlo_s[rows, :] = jnp.where(lo_half2, xs, 0.0).astype(BF)
        xhi_s[rows, :] = jnp.where(lo_half2, 0.0, xs).astype(BF)
        cm_s[rows, :] = act[:, GROUP_W + N:].astype(BF)
        bt_s[:, rows] = bm.T.astype(BF)
        col = col_ref[0, 0, rows, :]
        xdf_s[rows, :] = (xs * expand([column(col, r) for r in range(hpg)])).astype(BF)
        xdb_s[rows, :] = (xs * expand([column(col, hpg + r) for r in range(hpg)])).astype(BF)
        return carry

    lax.fori_loop(0, nc, conv_body, 0, unroll=8)

    def diag_body(c, carry):
        rows = pl.ds(pl.multiple_of(c * CHUNK, CHUNK), CHUNK)
        bm_t = bt_s[:, rows]
        cm = cm_s[rows, :]
        col = col_ref[0, 0, rows, :]
        row = row_ref[0, 0, :, rows]
        cb = jnp.dot(cm, bm_t, preferred_element_type=F32)
        csb_f = [column(col, 2 * hpg + r) for r in range(hpg)]
        csb_b = [column(col, 3 * hpg + r) for r in range(hpg)]
        ms = []
        for r in range(hpg):
            wf = (jnp.exp(jnp.where(li >= si, csb_f[r] - row[r:r + 1, :], NEG_BIG))
                  * row[2 * hpg + r:2 * hpg + r + 1, :])
            wb = (jnp.exp(jnp.where(li <= si, csb_b[r] - row[hpg + r:hpg + r + 1, :], NEG_BIG))
                  * row[3 * hpg + r:3 * hpg + r + 1, :])
            ms.append((cb * (wf + wb)).astype(BF))
        ys = []
        for p in range(2):
            lanes = slice(LANES * p, LANES * (p + 1))
            m2 = jnp.concatenate([ms[2 * p], ms[2 * p + 1]], axis=1)
            x2 = jnp.concatenate([xlo_s[rows, lanes], xhi_s[rows, lanes]], axis=0)
            ys.append(jnp.dot(m2, x2, preferred_element_type=F32))
        y_s[rows, :] = jnp.concatenate(ys, axis=1)
        sf_s[c] = jnp.dot(bm_t, xdf_s[rows, :], preferred_element_type=F32)
        sb_s[c] = jnp.dot(bm_t, xdb_s[rows, :], preferred_element_type=F32)
        ecs_f = jnp.exp(expand(csb_f))
        ecs_b = jnp.exp(expand(csb_b))
        ef_s[rows, :] = ecs_f.astype(BF)
        eb_s[rows, :] = ecs_b.astype(BF)
        edge_s[c, 0:1, :] = ecs_f[CHUNK - 1:CHUNK, :]
        edge_s[c, 1:2, :] = ecs_b[0:1, :]
        return carry

    lax.fori_loop(0, nc, diag_body, 0, unroll=8)

    def state_body(i, carry):
        hf, hb = carry
        j = nc - 1 - i
        s_f = sf_s[i]
        s_b = sb_s[j]
        sf_s[i] = hf
        sb_s[j] = hb
        return hf * edge_s[i, 0:1, :] + s_f, hb * edge_s[j, 1:2, :] + s_b

    zero_state = jnp.zeros((N, GROUP_W), F32)
    lax.fori_loop(0, nc, state_body, (zero_state, zero_state))

    def out_body(c, carry):
        rows = pl.ds(pl.multiple_of(c * CHUNK, CHUNK), CHUNK)
        cm = cm_s[rows, :]
        yoff = (jnp.dot(cm, sf_s[c].astype(BF), preferred_element_type=F32) * ef_s[rows, :].astype(F32)
                + jnp.dot(cm, sb_s[c].astype(BF), preferred_element_type=F32) * eb_s[rows, :].astype(F32))
        xs = (xlo_s[rows, :] + xhi_s[rows, :]).astype(F32)
        y = y_s[rows, :] + yoff + dsk_ref[0] * xs
        z = z_ref[0, rows, :].astype(F32)
        y = y * (z * _sigmoid(z))
        y_ref[0, rows, :] = _rms(y, nw_ref[0]).astype(BF)
        return carry

    lax.fori_loop(0, nc, out_body, 0, unroll=8)


def _shift_matrices():
    t = np.arange(CHUNK)[:, None]
    j = np.arange(CHUNK + 2 * CONV_HALO)[None, :]
    return np.concatenate([(j == t + CONV_HALO + k - 2) for k in (0, 1, 3, 4)], axis=0).astype(np.float32)


def _ssd(proj3, col_tbl, row_tbl, conv_w_g, conv_b_g, dskip_g, normw_g):
    B, S, _ = proj3.shape
    G, N, W = SSD_GROUPS, SSD_STATE, GROUP_W
    xs_blk = COL_XBC // W
    b_blk = (COL_XBC + SSD_D_INNER) // N
    c_blk = b_blk + G
    shift = jnp.asarray(_shift_matrices(), BF)
    return pl.pallas_call(
        _ssd_kernel,
        out_shape=jax.ShapeDtypeStruct((B, S, SSD_D_INNER), BF),
        grid=(B, G),
        in_specs=[
            pl.BlockSpec((1, S, W), lambda b, g: (b, 0, xs_blk + g)),
            pl.BlockSpec((1, S, N), lambda b, g: (b, 0, b_blk + g)),
            pl.BlockSpec((1, S, N), lambda b, g: (b, 0, c_blk + g)),
            pl.BlockSpec((1, S, W), lambda b, g: (b, 0, g)),
            pl.BlockSpec((1, 1, S, LANES), lambda b, g: (b, g, 0, 0)),
            pl.BlockSpec((1, 1, 4 * SSD_HPG, S), lambda b, g: (b, g, 0, 0)),
            pl.BlockSpec((1, 8, W + 2 * N), lambda b, g: (g, 0, 0)),
            pl.BlockSpec((1, 1, W + 2 * N), lambda b, g: (g, 0, 0)),
            pl.BlockSpec(shift.shape, lambda b, g: (0, 0)),
            pl.BlockSpec((1, 1, W), lambda b, g: (g, 0, 0)),
            pl.BlockSpec((1, 1, W), lambda b, g: (g, 0, 0)),
        ],
        out_specs=pl.BlockSpec((1, S, W), lambda b, g: (b, 0, g)),
        scratch_shapes=[
            pltpu.VMEM((S, W), BF),
            pltpu.VMEM((S, W), BF),
            pltpu.VMEM((S, N), BF),
            pltpu.VMEM((N, S), BF),
            pltpu.VMEM((S, W), BF),
            pltpu.VMEM((S, W), BF),
            pltpu.VMEM((S, W), F32),
            pltpu.VMEM((S // CHUNK, N, W), F32),
            pltpu.VMEM((S // CHUNK, N, W), F32),
            pltpu.VMEM((S, W), BF),
            pltpu.VMEM((S, W), BF),
            pltpu.VMEM((S // CHUNK, 8, W), F32),
        ],
        compiler_params=_cparams(("parallel", "parallel")),
        name="ssd",
    )(proj3, proj3, proj3, proj3, col_tbl, row_tbl, conv_w_g, conv_b_g, shift, dskip_g, normw_g)


def _merge_kernel(seq_len, ys_ref, xp_ref, xprev_ref, xnext_ref, gate1_ref, gate2_ref, x_ref,
                  wssd_ref, band_ref, wg_ref, pscale_ref, wpool_ref, wout_ref, fnw_ref,
                  h_ref, hnt_ref):
    tm = ys_ref.shape[0]
    i = pl.program_id(0)
    pos0 = (i * tm) % seq_len
    yssd = jnp.dot(ys_ref[...], wssd_ref[...], preferred_element_type=F32)

    cur = xp_ref[...]
    zero = jnp.zeros((POOL_HALO, POOL_WIDTH), BF)
    prev = jnp.where(pos0 > 0, xprev_ref[...], zero)
    nxt = jnp.where(pos0 + tm < seq_len, xnext_ref[...], zero)
    cat = jnp.concatenate([prev, cur, nxt], axis=0)
    pos = pos0 + lax.broadcasted_iota(jnp.int32, (tm, 1), 0)
    pooled = []
    for gi, w in enumerate(POOL_WINDOWS):
        cols = slice(gi * POOL_GDIM, (gi + 1) * POOL_GDIM)
        wsum = jnp.dot(band_ref[gi], cat[:, cols], preferred_element_type=F32)
        cnt = jnp.minimum(pos + w // 2, seq_len) - jnp.maximum(pos - w // 2, 0)
        pg = wsum / cnt.astype(F32) - cur[:, cols].astype(F32)
        pg = jnp.dot(pg.astype(BF), wg_ref[gi], preferred_element_type=F32)
        pooled.append((pg * pscale_ref[:, cols]).astype(BF))
    ypool = jnp.dot(jnp.concatenate(pooled, axis=1), wpool_ref[...], preferred_element_type=F32)

    merged = _sigmoid(gate1_ref[...].astype(F32)) * yssd + _sigmoid(gate2_ref[...].astype(F32)) * ypool
    h = x_ref[...] + jnp.dot(merged.astype(BF), wout_ref[...], preferred_element_type=F32)
    h_ref[...] = h
    hnt_ref[...] = _rms(h, fnw_ref[...]).T.astype(BF)


def _merge(y_ssd, proj, x2, w_ssd, band, w_g, pool_scale, w_pool, w_out, ffn_w, seq_len, tm=TM_MERGE):
    T, D = x2.shape
    hb = tm // POOL_HALO
    last_hb = T // POOL_HALO - 1
    pool_blk = COL_POOL // POOL_WIDTH
    g1_blk = COL_GATE // D
    return pl.pallas_call(
        functools.partial(_merge_kernel, seq_len),
        out_shape=(jax.ShapeDtypeStruct((T, D), F32), jax.ShapeDtypeStruct((D, T), BF)),
        grid=(T // tm,),
        in_specs=[
            pl.BlockSpec((tm, D), lambda i: (i, 0)),
            pl.BlockSpec((tm, POOL_WIDTH), lambda i: (i, pool_blk)),
            pl.BlockSpec((POOL_HALO, POOL_WIDTH), lambda i: (jnp.maximum(i * hb - 1, 0), pool_blk)),
            pl.BlockSpec((POOL_HALO, POOL_WIDTH), lambda i: (jnp.minimum((i + 1) * hb, last_hb), pool_blk)),
            pl.BlockSpec((tm, D), lambda i: (i, g1_blk)),
            pl.BlockSpec((tm, D), lambda i: (i, g1_blk + 1)),
            pl.BlockSpec((tm, D), lambda i: (i, 0)),
            _resident(w_ssd.shape),
            _resident(band.shape),
            _resident(w_g.shape),
            _resident(pool_scale.shape),
            _resident(w_pool.shape),
            _resident(w_out.shape),
            _resident(ffn_w.shape),
        ],
        out_specs=(
            pl.BlockSpec((tm, D), lambda i: (i, 0)),
            pl.BlockSpec((D, tm), lambda i: (0, i)),
        ),
        compiler_params=_cparams(("parallel",)),
        name="merge",
    )(y_ssd, proj, proj, proj, proj, proj, x2, w_ssd, band, w_g, pool_scale, w_pool, w_out, ffn_w)


def _cmpx(v, i, j):
    hi = jnp.maximum(v[i], v[j])
    lo = jnp.minimum(v[i], v[j])
    v[i], v[j] = hi, lo


def _bitonic_merge16(v):
    v = list(v)
    for j in (8, 4, 2, 1):
        for i in range(16):
            if i ^ j > i:
                _cmpx(v, i, i ^ j)
    return v


def _bitonic_sort16(v):
    v = list(v)
    for k in (2, 4, 8):
        for j in [s for s in (4, 2, 1) if s < k]:
            for i in range(16):
                l = i ^ j
                if l > i:
                    if i & k == 0:
                        _cmpx(v, i, l)
                    else:
                        _cmpx(v, l, i)
    return _bitonic_merge16(v)


def _top16_union(p, q):
    return _bitonic_merge16([jnp.maximum(p[i], q[15 - i]) for i in range(16)])


def _prefix_count(bs, pred):
    m8 = pred(bs[7])
    m4 = pred(jnp.where(m8, bs[11], bs[3]))
    m2 = pred(jnp.where(m8, jnp.where(m4, bs[13], bs[9]), jnp.where(m4, bs[5], bs[1])))
    lo = jnp.where(m4, jnp.where(m2, bs[6], bs[4]), jnp.where(m2, bs[2], bs[0]))
    hi = jnp.where(m4, jnp.where(m2, bs[14], bs[12]), jnp.where(m2, bs[10], bs[8]))
    m1 = pred(jnp.where(m8, hi, lo))
    cnt = (jnp.where(m8, 8.0, 0.0) + jnp.where(m4, 4.0, 0.0)) + (jnp.where(m2, 2.0, 0.0) + jnp.where(m1, 1.0, 0.0))
    return jnp.where(pred(bs[15]), 16.0, cnt)


def _route_kernel(hnt_ref, wq_ref, keys_ref, r1_ref, vh_ref, n_ref, u_ref, q_s, sc_s, ac_s, bc_s, tz_s):
    tm = hnt_ref.shape[1]
    nlb = tm // LANES
    K = PEER_TOPK
    qrows = 4 * PEER_HALF

    for r0 in range(0, q_s.shape[0], qrows):
        q_s[r0:r0 + qrows, :] = jnp.dot(wq_ref[r0:r0 + qrows, :], hnt_ref[...],
                                        preferred_element_type=F32).astype(BF)
    for hk in range(2 * PEER_HEADS):
        sc_s[hk] = jnp.dot(keys_ref[hk], q_s[hk * PEER_HALF:(hk + 1) * PEER_HALF, :],
                           preferred_element_type=F32)
    ac_s[...] = jnp.zeros_like(ac_s)
    bc_s[...] = jnp.zeros_like(bc_s)

    sub = lax.broadcasted_iota(jnp.int32, (8, LANES), 0)

    def top_body(idx, carry):
        h = idx // nlb
        lanes = pl.ds(pl.multiple_of((idx % nlb) * LANES, LANES), LANES)
        for half, dst in ((0, ac_s), (1, bc_s)):
            v = _bitonic_sort16([sc_s[2 * h + half, pl.ds(8 * k, 8), lanes] for k in range(PEER_KEYS // 8)])
            for sh in (4, 2, 1):
                v = _bitonic_merge16([jnp.maximum(v[i], pltpu.roll(v[15 - i], sh, axis=0)) for i in range(16)])
            for r in range(K):
                dst[r, :, lanes] = jnp.where(sub == h, v[r], dst[r, :, lanes])
        return carry

    lax.fori_loop(0, PEER_HEADS * nlb, top_body, 0, unroll=2)

    def pair_body(lb, carry):
        lanes = pl.ds(pl.multiple_of(lb * LANES, LANES), LANES)
        a = [ac_s[r, :, lanes] for r in range(K)]
        b = [bc_s[r, :, lanes] for r in range(K)]
        first = [a[0] + b[j] for j in range(K)]
        rest = [a[i] + b[j] for i in range(1, K) for j in range(K // (i + 1))]
        rest = rest + [jnp.full_like(a[0], -jnp.inf)] * (3 * K - len(rest))
        groups = [_bitonic_sort16(rest[K * g:K * (g + 1)]) for g in range(3)]
        top = _top16_union(_top16_union(first, groups[0]), _top16_union(groups[1], groups[2]))
        z = None
        for r in range(K):
            e = jnp.exp(top[r] - top[0])
            z = e if z is None else z + e
        tz_s[0, :, lanes] = top[K - 1]
        tz_s[1, :, lanes] = 1.0 / z
        return carry

    lax.fori_loop(0, nlb, pair_body, 0, unroll=2)

    def table_body(lb, carry):
        lanes = pl.ds(pl.multiple_of(lb * LANES, LANES), LANES)
        for h in range(PEER_HEADS):
            s0 = sc_s[2 * h, :, lanes]
            s1 = sc_s[2 * h + 1, :, lanes]
            tau = tz_s[0, h:h + 1, lanes]
            bs = [bc_s[r, h:h + 1, lanes] for r in range(K)]
            r1_ref[h, :, lanes] = (_prefix_count(bs, lambda b: b > s1) + 1.0).astype(BF)
            n_ref[h, :, lanes] = _prefix_count(bs, lambda b: s0 + b >= tau)
            vh_ref[h, :, lanes] = jnp.exp(s1 - bc_s[0, h:h + 1, lanes]).astype(BF)
            u_ref[h, :, lanes] = jnp.exp(s0 - ac_s[0, h:h + 1, lanes]) * tz_s[1, h:h + 1, lanes]
        return carry

    lax.fori_loop(0, nlb, table_body, 0)


def _route(hnt, wq_t, keys, tm=TM_ROUTE):
    D, T = hnt.shape
    H, NK = PEER_HEADS, PEER_KEYS
    tbl = lambda dt: jax.ShapeDtypeStruct((H, NK, T), dt)
    tspec = pl.BlockSpec((H, NK, tm), lambda i: (0, 0, i))
    return pl.pallas_call(
        _route_kernel,
        out_shape=(tbl(BF), tbl(BF), tbl(F32), tbl(F32)),
        grid=(T // tm,),
        in_specs=[
            pl.BlockSpec((D, tm), lambda i: (0, i)),
            _resident(wq_t.shape),
            _resident(keys.shape),
        ],
        out_specs=(tspec, tspec, tspec, tspec),
        scratch_shapes=[
            pltpu.VMEM((wq_t.shape[0], tm), BF),
            pltpu.VMEM((2 * H, NK, tm), F32),
            pltpu.VMEM((PEER_TOPK, H, tm), F32),
            pltpu.VMEM((PEER_TOPK, H, tm), F32),
            pltpu.VMEM((2, H, tm), F32),
        ],
        compiler_params=_cparams(("parallel",)),
        name="route",
    )(hnt, wq_t, keys)


def _experts_kernel(hnt_ref, u_ref, vt_ref, r1_ref, vh_ref, n_ref, uu_ref, h_ref, fw_ref,
                    o_ref, acc_s, coef_s):
    eb = u_ref.shape[0]
    tm = hnt_ref.shape[1]
    e = pl.program_id(1)

    @pl.when(e == 0)
    def _():
        acc_s[...] = jnp.zeros_like(acc_s)

    a_t = jnp.dot(u_ref[...], hnt_ref[...], preferred_element_type=F32)
    for ib in range(eb // PEER_KEYS):
        a = a_t[ib * PEER_KEYS:(ib + 1) * PEER_KEYS, :]
        ab = a.astype(BF)
        act = (0.5 * ab) * (1.0 + lax.erf(ab * math.sqrt(0.5)))
        gate = jnp.zeros((PEER_KEYS, tm), BF)
        for h in range(PEER_HEADS):
            nrow = jnp.broadcast_to(n_ref[h, ib:ib + 1, :], (PEER_KEYS, tm)).astype(BF)
            urow = jnp.broadcast_to(uu_ref[h, ib:ib + 1, :], (PEER_KEYS, tm)).astype(BF)
            sel = jnp.where(r1_ref[h] <= nrow, vh_ref[h], jnp.zeros((PEER_KEYS, tm), BF))
            gate = gate + sel * urow
        coef_s[ib * PEER_KEYS:(ib + 1) * PEER_KEYS, :] = act * gate
    acc_s[...] += jnp.dot(vt_ref[...], coef_s[...], preferred_element_type=F32)

    @pl.when(e == pl.num_programs(1) - 1)
    def _():
        o_ref[...] = _rms(h_ref[...] + acc_s[...].T, fw_ref[...])


def _experts(hnt, u_b, v_t, r1, vh, n, uu, h, final_w, tm=TM_EXPERTS):
    D, T = hnt.shape
    E = u_b.shape[0]
    H, NK = PEER_HEADS, PEER_KEYS
    kb = KEYS_PER_STEP
    eb = kb * NK
    tspec = pl.BlockSpec((H, NK, tm), lambda i, e: (0, 0, i), pipeline_mode=pl.Buffered(1))
    kspec = pl.BlockSpec((H, kb, tm), lambda i, e: (0, e, i))
    return pl.pallas_call(
        _experts_kernel,
        out_shape=jax.ShapeDtypeStruct((T, D), F32),
        grid=(T // tm, E // eb),
        in_specs=[
            pl.BlockSpec((D, tm), lambda i, e: (0, i), pipeline_mode=pl.Buffered(1)),
            pl.BlockSpec((eb, D), lambda i, e: (e, 0)),
            pl.BlockSpec((D, eb), lambda i, e: (0, e)),
            tspec, tspec, kspec, kspec,
            pl.BlockSpec((tm, D), lambda i, e: (i, 0), pipeline_mode=pl.Buffered(1)),
            pl.BlockSpec((1, D), lambda i, e: (0, 0)),
        ],
        out_specs=pl.BlockSpec((tm, D), lambda i, e: (i, 0)),
        scratch_shapes=[pltpu.VMEM((D, tm), F32), pltpu.VMEM((eb, tm), BF)],
        compiler_params=_cparams(("parallel", "arbitrary")),
        name="experts",
    )(hnt, u_b, v_t, r1, vh, n, uu, h, final_w)


def _band_matrices(tm):
    t = np.arange(tm)[:, None]
    j = np.arange(tm + 2 * POOL_HALO)[None, :]
    return np.stack([((j >= t + POOL_HALO - w // 2) & (j < t + POOL_HALO + w // 2)) for w in POOL_WINDOWS]
                    ).astype(np.float32)


def _tri_matrix():
    l = np.arange(CHUNK)[:, None]
    s = np.arange(CHUNK)[None, :]
    return np.concatenate([(s <= l), (s >= l)], axis=0).astype(np.float32)


def _dt_perm():
    cols = []
    for g in range(SSD_GROUPS):
        for d in range(2):
            for r in range(SSD_HPG):
                cols.append(d * SSD_HEADS + g * SSD_HPG + r)
    return np.asarray(cols)


def kernel(x, mixer_norm_w, w_in, conv_w, conv_b, dt_bias, a_log, d_skip, ssd_norm_w, w_ssd_branch, w_pool_group, pool_scale, w_pool_branch, w_out, ffn_norm_w, w_query, sub_keys, expert_u, expert_v, final_norm_w):
    B, S, D = x.shape
    T = B * S
    G, N, W = SSD_GROUPS, SSD_STATE, GROUP_W
    assert D == D_MODEL and mixer_norm_w.shape[0] == 1, "one layer of width D_MODEL"
    assert S % ROWS_DTPREP == 0 and S % TM_MERGE == 0, "dt_prep / merge tiles must not straddle sequences"
    assert all(T % t == 0 for t in (TM_INPROJ, TM_ROUTE, TM_EXPERTS))

    w_in0 = w_in[0]
    o1 = SSD_D_INNER
    o2 = o1 + SSD_XBC
    o3 = o2 + 2 * SSD_HEADS
    o4 = o3 + POOL_WIDTH
    w_main = jnp.concatenate([w_in0[:, :o2], w_in0[:, o4:], w_in0[:, o3:o4]], axis=1).astype(BF)
    perm = _dt_perm()
    pad = LANES - perm.size
    w_dt = jnp.pad(w_in0[:, o2:o3][:, perm], ((0, 0), (0, pad))).astype(BF)
    bias_row = jnp.pad(dt_bias[0].reshape(-1)[perm], (0, pad)).reshape(1, LANES)
    alog_row = jnp.pad(a_log[0].reshape(-1)[perm], (0, pad)).reshape(1, LANES)

    cw = conv_w[0, :, 0, :]
    cwx = cw[:, :SSD_D_INNER].reshape(SSD_CONV, G, W)
    cwb = cw[:, SSD_D_INNER:SSD_D_INNER + G * N].reshape(SSD_CONV, G, N)
    cwc = cw[:, SSD_D_INNER + G * N:].reshape(SSD_CONV, G, N)
    conv_w_g = jnp.pad(jnp.transpose(jnp.concatenate([cwx, cwb, cwc], axis=2), (1, 0, 2)),
                       ((0, 0), (0, 8 - SSD_CONV), (0, 0)))
    cbv = conv_b[0]
    conv_b_g = jnp.concatenate([cbv[:SSD_D_INNER].reshape(G, 1, W),
                                cbv[SSD_D_INNER:SSD_D_INNER + G * N].reshape(G, 1, N),
                                cbv[SSD_D_INNER + G * N:].reshape(G, 1, N)], axis=2)
    dskip_g = jnp.repeat(d_skip[0], SSD_HEAD_DIM).reshape(G, 1, W)
    normw_g = ssd_norm_w[0].reshape(G, 1, W)

    wq_t = w_query[0].T.astype(BF)
    keys = sub_keys[0].reshape(2 * PEER_HEADS, PEER_KEYS, PEER_HALF).astype(BF)
    u_b = expert_u[0].astype(BF)
    v_t = expert_v[0].T.astype(BF)

    x2 = x.reshape(T, D)
    proj, dt_raw = _in_proj(x2, mixer_norm_w, w_main, w_dt)
    dtcs, cst = _dt_prep(dt_raw.reshape(B, S, LANES), bias_row, alog_row, jnp.asarray(_tri_matrix(), BF))
    y_ssd = _ssd(proj.reshape(B, S, MAIN_W), dtcs, cst, conv_w_g, conv_b_g, dskip_g, normw_g)
    h, hnt = _merge(y_ssd.reshape(T, SSD_D_INNER), proj, x2,
                    w_ssd_branch[0].astype(BF), jnp.asarray(_band_matrices(TM_MERGE), BF),
                    w_pool_group[0].astype(BF), pool_scale, w_pool_branch[0].astype(BF),
                    w_out[0].astype(BF), ffn_norm_w, S)
    r1, vh, n, uu = _route(hnt, wq_t, keys)
    out = _experts(hnt, u_b, v_t, r1, vh, n, uu, h, final_norm_w.reshape(1, D))
    return out.reshape(B, S, D)
```

```python
import functools
import math

import numpy as np
import jax
import jax.numpy as jnp
from jax import lax
from jax.experimental import pallas as pl
from jax.experimental.pallas import tpu as pltpu

F32 = jnp.float32
BF = jnp.bfloat16

LANES = 128
V7X_VMEM_BYTES = 64 * 1024 * 1024
VMEM_LIMIT = V7X_VMEM_BYTES * 7 // 8

D_MODEL = 2048
SSD_HEAD_DIM = 64
SSD_D_INNER = D_MODEL
SSD_HEADS = SSD_D_INNER // SSD_HEAD_DIM
SSD_GROUPS = 8
SSD_HPG = SSD_HEADS // SSD_GROUPS
SSD_STATE = 128
SSD_CONV = 5
CHUNK = 128
SSD_XBC = SSD_D_INNER + 2 * SSD_GROUPS * SSD_STATE
GROUP_W = SSD_HPG * SSD_HEAD_DIM
POOL_WIDTH = D_MODEL // 2
POOL_WINDOWS = (2, 4, 8, 16)
POOL_GDIM = POOL_WIDTH // len(POOL_WINDOWS)
POOL_HALO = 64
CONV_HALO = 64
PEER_HEADS = 8
PEER_KEYS = 128
PEER_HALF = 128
PEER_TOPK = 16
EPS = 1e-6
NEG_BIG = -1e30

TM_INPROJ = 1024
TN_INPROJ = 1024
ROWS_DTPREP = 4 * CHUNK
TM_MERGE = 256
TM_ROUTE = 512
TM_EXPERTS = 512
KEYS_PER_STEP = 8

MAIN_W = SSD_D_INNER + SSD_XBC + 2 * D_MODEL + POOL_WIDTH
COL_XBC = SSD_D_INNER
COL_GATE = COL_XBC + SSD_XBC
COL_POOL = COL_GATE + 2 * D_MODEL


def _sigmoid(v):
    return 1.0 / (1.0 + jnp.exp(-v))


def _rms(v, w):
    return v * lax.rsqrt(jnp.mean(v * v, axis=-1, keepdims=True) + EPS) * w


def _cparams(sem):
    return pltpu.CompilerParams(dimension_semantics=sem, vmem_limit_bytes=VMEM_LIMIT)


def _resident(shape):
    nd = len(shape)
    return pl.BlockSpec(shape, lambda *_: (0,) * nd, pipeline_mode=pl.Buffered(1))


def _inproj_kernel(x_ref, nw_ref, w_ref, wdt_ref, o_ref, dt_ref, xn_ref):
    tm = x_ref.shape[0]

    @pl.when(pl.program_id(1) == 0)
    def _():
        def body(r, carry):
            rows = pl.ds(pl.multiple_of(r * CHUNK, CHUNK), CHUNK)
            xn = _rms(x_ref[rows, :], nw_ref[...]).astype(BF)
            xn_ref[rows, :] = xn
            dt_ref[rows, :] = jnp.dot(xn, wdt_ref[...], preferred_element_type=F32)
            return carry

        lax.fori_loop(0, tm // CHUNK, body, 0, unroll=2)

    o_ref[...] = jnp.dot(xn_ref[...], w_ref[...], preferred_element_type=F32).astype(BF)


def _in_proj(x2, norm_w, w_main, w_dt, tm=TM_INPROJ, tn=TN_INPROJ):
    T, D = x2.shape
    N = w_main.shape[1]
    return pl.pallas_call(
        _inproj_kernel,
        out_shape=(jax.ShapeDtypeStruct((T, N), BF), jax.ShapeDtypeStruct((T, LANES), F32)),
        grid=(T // tm, N // tn),
        in_specs=[
            pl.BlockSpec((tm, D), lambda i, j: (i, 0)),
            pl.BlockSpec((1, D), lambda i, j: (0, 0)),
            pl.BlockSpec((D, tn), lambda i, j: (0, j)),
            pl.BlockSpec((D, LANES), lambda i, j: (0, 0)),
        ],
        out_specs=(
            pl.BlockSpec((tm, tn), lambda i, j: (i, j)),
            pl.BlockSpec((tm, LANES), lambda i, j: (i, 0)),
        ),
        scratch_shapes=[pltpu.VMEM((tm, D), BF)],
        compiler_params=_cparams(("parallel", "arbitrary")),
        name="in_proj",
    )(x2, norm_w, w_main, w_dt)


def _split3(a):
    hi = a.astype(BF)
    r1 = a - hi.astype(F32)
    mid = r1.astype(BF)
    lo = (r1 - mid.astype(F32)).astype(BF)
    return hi, mid, lo


def _dtprep_kernel(raw_ref, bias_ref, alog_ref, tri_ref, col_ref, row_ref):
    nh = 2 * SSD_HPG
    lane = lax.broadcasted_iota(jnp.int32, (CHUNK, LANES), 1)
    backward = (lane % nh) >= SSD_HPG
    tri = tri_ref[...]
    for ci in range(raw_ref.shape[0] // CHUNK):
        rows = slice(ci * CHUNK, (ci + 1) * CHUNK)
        v = raw_ref[rows, :] + bias_ref[...]
        dt = jnp.maximum(v, 0.0) + jnp.log(1.0 + jnp.exp(-jnp.abs(v)))
        a = dt * (-jnp.exp(alog_ref[...]))
        cs2 = None
        for part in _split3(a):
            t = jnp.dot(tri, part, preferred_element_type=F32)
            cs2 = t if cs2 is None else cs2 + t
        cs = jnp.where(backward, cs2[CHUNK:, :], cs2[:CHUNK, :])
        edge = jnp.where(backward[0:1, :], cs[0:1, :], cs[CHUNK - 1:CHUNK, :])
        w = dt * jnp.exp(edge - cs)
        cs_t = cs.T
        dt_t = dt.T
        for g in range(SSD_GROUPS):
            wg = w if g == 0 else pltpu.roll(w, (LANES - nh * g) % LANES, axis=1)
            cg = pltpu.roll(cs, (LANES + nh - nh * g) % LANES, axis=1)
            col_ref[0, g, rows, :] = jnp.where(lane < nh, wg, jnp.where(lane < 2 * nh, cg, 0.0))
            row_ref[0, g, 0:nh, rows] = cs_t[nh * g:nh * (g + 1), :]
            row_ref[0, g, nh:2 * nh, rows] = dt_t[nh * g:nh * (g + 1), :]


def _dt_prep(dt_raw3, bias_row, alog_row, tri, rows=ROWS_DTPREP):
    B, S, _ = dt_raw3.shape
    nh = 2 * SSD_HPG
    return pl.pallas_call(
        _dtprep_kernel,
        out_shape=(jax.ShapeDtypeStruct((B, SSD_GROUPS, S, LANES), F32),
                   jax.ShapeDtypeStruct((B, SSD_GROUPS, 2 * nh, S), F32)),
        grid=(B, S // rows),
        in_specs=[
            pl.BlockSpec((None, rows, LANES), lambda b, c: (b, c, 0)),
            pl.BlockSpec((1, LANES), lambda b, c: (0, 0)),
            pl.BlockSpec((1, LANES), lambda b, c: (0, 0)),
            pl.BlockSpec((2 * CHUNK, CHUNK), lambda b, c: (0, 0)),
        ],
        out_specs=(
            pl.BlockSpec((1, SSD_GROUPS, rows, LANES), lambda b, c: (b, 0, c, 0)),
            pl.BlockSpec((1, SSD_GROUPS, 2 * nh, rows), lambda b, c: (b, 0, 0, c)),
        ),
        compiler_params=_cparams(("parallel", "parallel")),
        name="dt_prep",
    )(dt_raw3, bias_row, alog_row, tri)


def _ssd_kernel(xs_ref, b_ref, c_ref, z_ref, col_ref, row_ref, cw_ref, cb_ref, shift_ref, dsk_ref, nw_ref,
                y_ref, xlo_s, xhi_s, cm_s, bt_s, xdf_s, xdb_s, y_s, sf_s, sb_s, ef_s, eb_s, edge_s):
    S = xs_ref.shape[1]
    nc = S // CHUNK
    H = CONV_HALO
    N = SSD_STATE
    hpg = SSD_HPG

    li = lax.broadcasted_iota(jnp.int32, (CHUNK, CHUNK), 0)
    si = lax.broadcasted_iota(jnp.int32, (CHUNK, CHUNK), 1)
    lo_half = si < SSD_HEAD_DIM
    lo_half2 = jnp.concatenate([lo_half, lo_half], axis=1)

    def expand(bc):
        return jnp.concatenate([jnp.where(lo_half, bc[0], bc[1]), jnp.where(lo_half, bc[2], bc[3])], axis=1)

    def column(tile, k):
        return jnp.broadcast_to(tile[:, k:k + 1], (CHUNK, LANES))

    def halo_rows(src_ref, c):
        r0 = pl.multiple_of(c * CHUNK, CHUNK)
        cur = src_ref[0, pl.ds(r0, CHUNK), :]
        prev = src_ref[0, pl.ds(pl.multiple_of(jnp.maximum(r0 - H, 0), H), H), :]
        nxt = src_ref[0, pl.ds(pl.multiple_of(jnp.minimum(r0 + CHUNK, S - H), H), H), :]
        prev = jnp.where(c > 0, prev, jnp.zeros_like(prev))
        nxt = jnp.where(c < nc - 1, nxt, jnp.zeros_like(nxt))
        return jnp.concatenate([prev, cur, nxt], axis=0), cur

    def conv_body(c, carry):
        rows = pl.ds(pl.multiple_of(c * CHUNK, CHUNK), CHUNK)
        parts = [halo_rows(r, c) for r in (xs_ref, b_ref, c_ref)]
        cat = jnp.concatenate([p[0] for p in parts], axis=1)
        cur = jnp.concatenate([p[1] for p in parts], axis=1).astype(F32)
        sh = jnp.dot(shift_ref[...], cat, preferred_element_type=F32)
        w = cw_ref[0]
        acc = cb_ref[0] + w[2:3, :] * cur
        for j, k in enumerate((0, 1, 3, 4)):
            acc = acc + w[k:k + 1, :] * sh[j * CHUNK:(j + 1) * CHUNK, :]
        act = acc * _sigmoid(acc)
        xs = act[:, :GROUP_W]
        bm = act[:, GROUP_W:GROUP_W + N]
        xlo_s[rows, :] = jnp.where(lo_half2, xs, 0.0).astype(BF)
        xhi_s[rows, :] = jnp.where(lo_half2, 0.0, xs).astype(BF)
        cm_s[rows, :] = act[:, GROUP_W + N:].astype(BF)
        bt_s[:, rows] = bm.T.astype(BF)
        col = col_ref[0, 0, rows, :]
        xdf_s[rows, :] = (xs * expand([column(col, r) for r in range(hpg)])).astype(BF)
        xdb_s[rows, :] = (xs * expand([column(col, hpg + r) for r in range(hpg)])).astype(BF)
        return carry

    lax.fori_loop(0, nc, conv_body, 0, unroll=16)

    def diag_body(c, carry):
        rows = pl.ds(pl.multiple_of(c * CHUNK, CHUNK), CHUNK)
        bm_t = bt_s[:, rows]
        cm = cm_s[rows, :]
        col = col_ref[0, 0, rows, :]
        row = row_ref[0, 0, :, rows]
        cb = jnp.dot(cm, bm_t, preferred_element_type=F32)
        csb_f = [column(col, 2 * hpg + r) for r in range(hpg)]
        csb_b = [column(col, 3 * hpg + r) for r in range(hpg)]
        ms = []
        for r in range(hpg):
            wf = (jnp.exp(jnp.where(li >= si, csb_f[r] - row[r:r + 1, :], NEG_BIG))
                  * row[2 * hpg + r:2 * hpg + r + 1, :])
            wb = (jnp.exp(jnp.where(li <= si, csb_b[r] - row[hpg + r:hpg + r + 1, :], NEG_BIG))
                  * row[3 * hpg + r:3 * hpg + r + 1, :])
            ms.append((cb * (wf + wb)).astype(BF))
        ys = []
        for p in range(2):
            lanes = slice(LANES * p, LANES * (p + 1))
            m2 = jnp.concatenate([ms[2 * p], ms[2 * p + 1]], axis=1)
            x2 = jnp.concatenate([xlo_s[rows, lanes], xhi_s[rows, lanes]], axis=0)
            ys.append(jnp.dot(m2, x2, preferred_element_type=F32))
        y_s[rows, :] = jnp.concatenate(ys, axis=1)
        sf_s[c] = jnp.dot(bm_t, xdf_s[rows, :], preferred_element_type=F32)
        sb_s[c] = jnp.dot(bm_t, xdb_s[rows, :], preferred_element_type=F32)
        ecs_f = jnp.exp(expand(csb_f))
        ecs_b = jnp.exp(expand(csb_b))
        ef_s[rows, :] = ecs_f.astype(BF)
        eb_s[rows, :] = ecs_b.astype(BF)
        edge_s[c, 0:1, :] = ecs_f[CHUNK - 1:CHUNK, :]
        edge_s[c, 1:2, :] = ecs_b[0:1, :]
        return carry

    lax.fori_loop(0, nc, diag_body, 0, unroll=16)

    def state_body(i, carry):
        hf, hb = carry
        j = nc - 1 - i
        s_f = sf_s[i]
        s_b = sb_s[j]
        sf_s[i] = hf
        sb_s[j] = hb
        return hf * edge_s[i, 0:1, :] + s_f, hb * edge_s[j, 1:2, :] + s_b

    zero_state = jnp.zeros((N, GROUP_W), F32)
    lax.fori_loop(0, nc, state_body, (zero_state, zero_state))

    def out_body(c, carry):
        rows = pl.ds(pl.multiple_of(c * CHUNK, CHUNK), CHUNK)
        cm = cm_s[rows, :]
        yoff = (jnp.dot(cm, sf_s[c].astype(BF), preferred_element_type=F32) * ef_s[rows, :].astype(F32)
                + jnp.dot(cm, sb_s[c].astype(BF), preferred_element_type=F32) * eb_s[rows, :].astype(F32))
        xs = (xlo_s[rows, :] + xhi_s[rows, :]).astype(F32)
        y = y_s[rows, :] + yoff + dsk_ref[0] * xs
        z = z_ref[0, rows, :].astype(F32)
        y = y * (z * _sigmoid(z))
        y_ref[0, rows, :] = _rms(y, nw_ref[0]).astype(BF)
        return carry

    lax.fori_loop(0, nc, out_body, 0, unroll=16)


def _shift_matrices():
    t = np.arange(CHUNK)[:, None]
    j = np.arange(CHUNK + 2 * CONV_HALO)[None, :]
    return np.concatenate([(j == t + CONV_HALO + k - 2) for k in (0, 1, 3, 4)], axis=0).astype(np.float32)


def _ssd(proj3, col_tbl, row_tbl, conv_w_g, conv_b_g, dskip_g, normw_g):
    B, S, _ = proj3.shape
    G, N, W = SSD_GROUPS, SSD_STATE, GROUP_W
    xs_blk = COL_XBC // W
    b_blk = (COL_XBC + SSD_D_INNER) // N
    c_blk = b_blk + G
    shift = jnp.asarray(_shift_matrices(), BF)
    return pl.pallas_call(
        _ssd_kernel,
        out_shape=jax.ShapeDtypeStruct((B, S, SSD_D_INNER), BF),
        grid=(B, G),
        in_specs=[
            pl.BlockSpec((1, S, W), lambda b, g: (b, 0, xs_blk + g)),
            pl.BlockSpec((1, S, N), lambda b, g: (b, 0, b_blk + g)),
            pl.BlockSpec((1, S, N), lambda b, g: (b, 0, c_blk + g)),
            pl.BlockSpec((1, S, W), lambda b, g: (b, 0, g)),
            pl.BlockSpec((1, 1, S, LANES), lambda b, g: (b, g, 0, 0)),
            pl.BlockSpec((1, 1, 4 * SSD_HPG, S), lambda b, g: (b, g, 0, 0)),
            pl.BlockSpec((1, 8, W + 2 * N), lambda b, g: (g, 0, 0)),
            pl.BlockSpec((1, 1, W + 2 * N), lambda b, g: (g, 0, 0)),
            pl.BlockSpec(shift.shape, lambda b, g: (0, 0)),
            pl.BlockSpec((1, 1, W), lambda b, g: (g, 0, 0)),
            pl.BlockSpec((1, 1, W), lambda b, g: (g, 0, 0)),
        ],
        out_specs=pl.BlockSpec((1, S, W), lambda b, g: (b, 0, g)),
        scratch_shapes=[
            pltpu.VMEM((S, W), BF),
            pltpu.VMEM((S, W), BF),
            pltpu.VMEM((S, N), BF),
            pltpu.VMEM((N, S), BF),
            pltpu.VMEM((S, W), BF),
            pltpu.VMEM((S, W), BF),
            pltpu.VMEM((S, W), F32),
            pltpu.VMEM((S // CHUNK, N, W), F32),
            pltpu.VMEM((S // CHUNK, N, W), F32),
            pltpu.VMEM((S, W), BF),
            pltpu.VMEM((S, W), BF),
            pltpu.VMEM((S // CHUNK, 8, W), F32),
        ],
        compiler_params=_cparams(("parallel", "parallel")),
        name="ssd",
    )(proj3, proj3, proj3, proj3, col_tbl, row_tbl, conv_w_g, conv_b_g, shift, dskip_g, normw_g)


def _merge_kernel(seq_len, ys_ref, xp_ref, xprev_ref, xnext_ref, gate1_ref, gate2_ref, x_ref,
                  wssd_ref, band_ref, wg_ref, pscale_ref, wpool_ref, wout_ref, fnw_ref,
                  h_ref, hnt_ref):
    tm = ys_ref.shape[0]
    i = pl.program_id(0)
    pos0 = (i * tm) % seq_len
    yssd = jnp.dot(ys_ref[...], wssd_ref[...], preferred_element_type=F32)

    cur = xp_ref[...]
    zero = jnp.zeros((POOL_HALO, POOL_WIDTH), BF)
    prev = jnp.where(pos0 > 0, xprev_ref[...], zero)
    nxt = jnp.where(pos0 + tm < seq_len, xnext_ref[...], zero)
    cat = jnp.concatenate([prev, cur, nxt], axis=0)
    pos = pos0 + lax.broadcasted_iota(jnp.int32, (tm, 1), 0)
    pooled = []
    for gi, w in enumerate(POOL_WINDOWS):
        cols = slice(gi * POOL_GDIM, (gi + 1) * POOL_GDIM)
        wsum = jnp.dot(band_ref[gi], cat[:, cols], preferred_element_type=F32)
        cnt = jnp.minimum(pos + w // 2, seq_len) - jnp.maximum(pos - w // 2, 0)
        pg = wsum / cnt.astype(F32) - cur[:, cols].astype(F32)
        pg = jnp.dot(pg.astype(BF), wg_ref[gi], preferred_element_type=F32)
        pooled.append((pg * pscale_ref[:, cols]).astype(BF))
    ypool = jnp.dot(jnp.concatenate(pooled, axis=1), wpool_ref[...], preferred_element_type=F32)

    merged = _sigmoid(gate1_ref[...].astype(F32)) * yssd + _sigmoid(gate2_ref[...].astype(F32)) * ypool
    h = x_ref[...] + jnp.dot(merged.astype(BF), wout_ref[...], preferred_element_type=F32)
    h_ref[...] = h
    hnt_ref[...] = _rms(h, fnw_ref[...]).T.astype(BF)


def _merge(y_ssd, proj, x2, w_ssd, band, w_g, pool_scale, w_pool, w_out, ffn_w, seq_len, tm=TM_MERGE):
    T, D = x2.shape
    hb = tm // POOL_HALO
    last_hb = T // POOL_HALO - 1
    pool_blk = COL_POOL // POOL_WIDTH
    g1_blk = COL_GATE // D
    return pl.pallas_call(
        functools.partial(_merge_kernel, seq_len),
        out_shape=(jax.ShapeDtypeStruct((T, D), F32), jax.ShapeDtypeStruct((D, T), BF)),
        grid=(T // tm,),
        in_specs=[
            pl.BlockSpec((tm, D), lambda i: (i, 0)),
            pl.BlockSpec((tm, POOL_WIDTH), lambda i: (i, pool_blk)),
            pl.BlockSpec((POOL_HALO, POOL_WIDTH), lambda i: (jnp.maximum(i * hb - 1, 0), pool_blk)),
            pl.BlockSpec((POOL_HALO, POOL_WIDTH), lambda i: (jnp.minimum((i + 1) * hb, last_hb), pool_blk)),
            pl.BlockSpec((tm, D), lambda i: (i, g1_blk)),
            pl.BlockSpec((tm, D), lambda i: (i, g1_blk + 1)),
            pl.BlockSpec((tm, D), lambda i: (i, 0)),
            _resident(w_ssd.shape),
            _resident(band.shape),
            _resident(w_g.shape),
            _resident(pool_scale.shape),
            _resident(w_pool.shape),
            _resident(w_out.shape),
            _resident(ffn_w.shape),
        ],
        out_specs=(
            pl.BlockSpec((tm, D), lambda i: (i, 0)),
            pl.BlockSpec((D, tm), lambda i: (0, i)),
        ),
        compiler_params=_cparams(("parallel",)),
        name="merge",
    )(y_ssd, proj, proj, proj, proj, proj, x2, w_ssd, band, w_g, pool_scale, w_pool, w_out, ffn_w)


def _cmpx(v, i, j):
    hi = jnp.maximum(v[i], v[j])
    lo = jnp.minimum(v[i], v[j])
    v[i], v[j] = hi, lo


def _bitonic_merge16(v):
    v = list(v)
    for j in (8, 4, 2, 1):
        for i in range(16):
            if i ^ j > i:
                _cmpx(v, i, i ^ j)
    return v


def _bitonic_sort16(v):
    v = list(v)
    for k in (2, 4, 8):
        for j in [s for s in (4, 2, 1) if s < k]:
            for i in range(16):
                l = i ^ j
                if l > i:
                    if i & k == 0:
                        _cmpx(v, i, l)
                    else:
                        _cmpx(v, l, i)
    return _bitonic_merge16(v)


def _top16_union(p, q):
    return _bitonic_merge16([jnp.maximum(p[i], q[15 - i]) for i in range(16)])


def _prefix_count(bs, pred):
    m8 = pred(bs[7])
    m4 = pred(jnp.where(m8, bs[11], bs[3]))
    m2 = pred(jnp.where(m8, jnp.where(m4, bs[13], bs[9]), jnp.where(m4, bs[5], bs[1])))
    lo = jnp.where(m4, jnp.where(m2, bs[6], bs[4]), jnp.where(m2, bs[2], bs[0]))
    hi = jnp.where(m4, jnp.where(m2, bs[14], bs[12]), jnp.where(m2, bs[10], bs[8]))
    m1 = pred(jnp.where(m8, hi, lo))
    cnt = (jnp.where(m8, 8.0, 0.0) + jnp.where(m4, 4.0, 0.0)) + (jnp.where(m2, 2.0, 0.0) + jnp.where(m1, 1.0, 0.0))
    return jnp.where(pred(bs[15]), 16.0, cnt)


def _route_kernel(hnt_ref, wq_ref, keys_ref, r1_ref, vh_ref, n_ref, u_ref, q_s, sc_s, ac_s, bc_s, tz_s):
    tm = hnt_ref.shape[1]
    nlb = tm // LANES
    K = PEER_TOPK
    qrows = 4 * PEER_HALF

    for r0 in range(0, q_s.shape[0], qrows):
        q_s[r0:r0 + qrows, :] = jnp.dot(wq_ref[r0:r0 + qrows, :], hnt_ref[...],
                                        preferred_element_type=F32).astype(BF)
    for hk in range(2 * PEER_HEADS):
        sc_s[hk] = jnp.dot(keys_ref[hk], q_s[hk * PEER_HALF:(hk + 1) * PEER_HALF, :],
                           preferred_element_type=F32)
    ac_s[...] = jnp.zeros_like(ac_s)
    bc_s[...] = jnp.zeros_like(bc_s)

    sub = lax.broadcasted_iota(jnp.int32, (8, LANES), 0)

    def top_body(idx, carry):
        h = idx // nlb
        lanes = pl.ds(pl.multiple_of((idx % nlb) * LANES, LANES), LANES)
        for half, dst in ((0, ac_s), (1, bc_s)):
            v = _bitonic_sort16([sc_s[2 * h + half, pl.ds(8 * k, 8), lanes] for k in range(PEER_KEYS // 8)])
            for sh in (4, 2, 1):
                v = _bitonic_merge16([jnp.maximum(v[i], pltpu.roll(v[15 - i], sh, axis=0)) for i in range(16)])
            for r in range(K):
                dst[r, :, lanes] = jnp.where(sub == h, v[r], dst[r, :, lanes])
        return carry

    lax.fori_loop(0, PEER_HEADS * nlb, top_body, 0, unroll=2)

    def pair_body(lb, carry):
        lanes = pl.ds(pl.multiple_of(lb * LANES, LANES), LANES)
        a = [ac_s[r, :, lanes] for r in range(K)]
        b = [bc_s[r, :, lanes] for r in range(K)]
        first = [a[0] + b[j] for j in range(K)]
        rest = [a[i] + b[j] for i in range(1, K) for j in range(K // (i + 1))]
        rest = rest + [jnp.full_like(a[0], -jnp.inf)] * (3 * K - len(rest))
        groups = [_bitonic_sort16(rest[K * g:K * (g + 1)]) for g in range(3)]
        top = _top16_union(_top16_union(first, groups[0]), _top16_union(groups[1], groups[2]))
        z = None
        for r in range(K):
            e = jnp.exp(top[r] - top[0])
            z = e if z is None else z + e
        tz_s[0, :, lanes] = top[K - 1]
        tz_s[1, :, lanes] = 1.0 / z
        return carry

    lax.fori_loop(0, nlb, pair_body, 0, unroll=2)

    def table_body(lb, carry):
        lanes = pl.ds(pl.multiple_of(lb * LANES, LANES), LANES)
        for h in range(PEER_HEADS):
            s0 = sc_s[2 * h, :, lanes]
            s1 = sc_s[2 * h + 1, :, lanes]
            tau = tz_s[0, h:h + 1, lanes]
            bs = [bc_s[r, h:h + 1, lanes] for r in range(K)]
            r1_ref[h, :, lanes] = (_prefix_count(bs, lambda b: b > s1) + 1.0).astype(BF)
            n_ref[h, :, lanes] = _prefix_count(bs, lambda b: s0 + b >= tau)
            vh_ref[h, :, lanes] = jnp.exp(s1 - bc_s[0, h:h + 1, lanes]).astype(BF)
            u_ref[h, :, lanes] = jnp.exp(s0 - ac_s[0, h:h + 1, lanes]) * tz_s[1, h:h + 1, lanes]
        return carry

    lax.fori_loop(0, nlb, table_body, 0)


def _route(hnt, wq_t, keys, tm=TM_ROUTE):
    D, T = hnt.shape
    H, NK = PEER_HEADS, PEER_KEYS
    tbl = lambda dt: jax.ShapeDtypeStruct((H, NK, T), dt)
    tspec = pl.BlockSpec((H, NK, tm), lambda i: (0, 0, i))
    return pl.pallas_call(
        _route_kernel,
        out_shape=(tbl(BF), tbl(BF), tbl(F32), tbl(F32)),
        grid=(T // tm,),
        in_specs=[
            pl.BlockSpec((D, tm), lambda i: (0, i)),
            _resident(wq_t.shape),
            _resident(keys.shape),
        ],
        out_specs=(tspec, tspec, tspec, tspec),
        scratch_shapes=[
            pltpu.VMEM((wq_t.shape[0], tm), BF),
            pltpu.VMEM((2 * H, NK, tm), F32),
            pltpu.VMEM((PEER_TOPK, H, tm), F32),
            pltpu.VMEM((PEER_TOPK, H, tm), F32),
            pltpu.VMEM((2, H, tm), F32),
        ],
        compiler_params=_cparams(("parallel",)),
        name="route",
    )(hnt, wq_t, keys)


def _experts_kernel(hnt_ref, u_ref, vt_ref, r1_ref, vh_ref, n_ref, uu_ref, h_ref, fw_ref,
                    o_ref, acc_s, coef_s):
    eb = u_ref.shape[0]
    tm = hnt_ref.shape[1]
    e = pl.program_id(1)

    @pl.when(e == 0)
    def _():
        acc_s[...] = jnp.zeros_like(acc_s)

    a_t = jnp.dot(u_ref[...], hnt_ref[...], preferred_element_type=F32)
    for ib in range(eb // PEER_KEYS):
        a = a_t[ib * PEER_KEYS:(ib + 1) * PEER_KEYS, :]
        ab = a.astype(BF)
        act = (0.5 * ab) * (1.0 + lax.erf(ab * math.sqrt(0.5)))
        gate = jnp.zeros((PEER_KEYS, tm), BF)
        for h in range(PEER_HEADS):
            nrow = jnp.broadcast_to(n_ref[h, ib:ib + 1, :], (PEER_KEYS, tm)).astype(BF)
            urow = jnp.broadcast_to(uu_ref[h, ib:ib + 1, :], (PEER_KEYS, tm)).astype(BF)
            sel = jnp.where(r1_ref[h] <= nrow, vh_ref[h], jnp.zeros((PEER_KEYS, tm), BF))
            gate = gate + sel * urow
        coef_s[ib * PEER_KEYS:(ib + 1) * PEER_KEYS, :] = act * gate
    acc_s[...] += jnp.dot(vt_ref[...], coef_s[...], preferred_element_type=F32)

    @pl.when(e == pl.num_programs(1) - 1)
    def _():
        o_ref[...] = _rms(h_ref[...] + acc_s[...].T, fw_ref[...])


def _experts(hnt, u_b, v_t, r1, vh, n, uu, h, final_w, tm=TM_EXPERTS):
    D, T = hnt.shape
    E = u_b.shape[0]
    H, NK = PEER_HEADS, PEER_KEYS
    kb = KEYS_PER_STEP
    eb = kb * NK
    tspec = pl.BlockSpec((H, NK, tm), lambda i, e: (0, 0, i), pipeline_mode=pl.Buffered(1))
    kspec = pl.BlockSpec((H, kb, tm), lambda i, e: (0, e, i))
    return pl.pallas_call(
        _experts_kernel,
        out_shape=jax.ShapeDtypeStruct((T, D), F32),
        grid=(T // tm, E // eb),
        in_specs=[
            pl.BlockSpec((D, tm), lambda i, e: (0, i), pipeline_mode=pl.Buffered(1)),
            pl.BlockSpec((eb, D), lambda i, e: (e, 0)),
            pl.BlockSpec((D, eb), lambda i, e: (0, e)),
            tspec, tspec, kspec, kspec,
            pl.BlockSpec((tm, D), lambda i, e: (i, 0), pipeline_mode=pl.Buffered(1)),
            pl.BlockSpec((1, D), lambda i, e: (0, 0)),
        ],
        out_specs=pl.BlockSpec((tm, D), lambda i, e: (i, 0)),
        scratch_shapes=[pltpu.VMEM((D, tm), F32), pltpu.VMEM((eb, tm), BF)],
        compiler_params=_cparams(("parallel", "arbitrary")),
        name="experts",
    )(hnt, u_b, v_t, r1, vh, n, uu, h, final_w)


def _band_matrices(tm):
    t = np.arange(tm)[:, None]
    j = np.arange(tm + 2 * POOL_HALO)[None, :]
    return np.stack([((j >= t + POOL_HALO - w // 2) & (j < t + POOL_HALO + w // 2)) for w in POOL_WINDOWS]
                    ).astype(np.float32)


def _tri_matrix():
    l = np.arange(CHUNK)[:, None]
    s = np.arange(CHUNK)[None, :]
    return np.concatenate([(s <= l), (s >= l)], axis=0).astype(np.float32)


def _dt_perm():
    cols = []
    for g in range(SSD_GROUPS):
        for d in range(2):
            for r in range(SSD_HPG):
                cols.append(d * SSD_HEADS + g * SSD_HPG + r)
    return np.asarray(cols)


def kernel(x, mixer_norm_w, w_in, conv_w, conv_b, dt_bias, a_log, d_skip, ssd_norm_w, w_ssd_branch, w_pool_group, pool_scale, w_pool_branch, w_out, ffn_norm_w, w_query, sub_keys, expert_u, expert_v, final_norm_w):
    B, S, D = x.shape
    T = B * S
    G, N, W = SSD_GROUPS, SSD_STATE, GROUP_W
    assert D == D_MODEL and mixer_norm_w.shape[0] == 1, "one layer of width D_MODEL"
    assert S % ROWS_DTPREP == 0 and S % TM_MERGE == 0, "dt_prep / merge tiles must not straddle sequences"
    assert all(T % t == 0 for t in (TM_INPROJ, TM_ROUTE, TM_EXPERTS))

    w_in0 = w_in[0]
    o1 = SSD_D_INNER
    o2 = o1 + SSD_XBC
    o3 = o2 + 2 * SSD_HEADS
    o4 = o3 + POOL_WIDTH
    w_main = jnp.concatenate([w_in0[:, :o2], w_in0[:, o4:], w_in0[:, o3:o4]], axis=1).astype(BF)
    perm = _dt_perm()
    pad = LANES - perm.size
    w_dt = jnp.pad(w_in0[:, o2:o3][:, perm], ((0, 0), (0, pad))).astype(BF)
    bias_row = jnp.pad(dt_bias[0].reshape(-1)[perm], (0, pad)).reshape(1, LANES)
    alog_row = jnp.pad(a_log[0].reshape(-1)[perm], (0, pad)).reshape(1, LANES)

    cw = conv_w[0, :, 0, :]
    cwx = cw[:, :SSD_D_INNER].reshape(SSD_CONV, G, W)
    cwb = cw[:, SSD_D_INNER:SSD_D_INNER + G * N].reshape(SSD_CONV, G, N)
    cwc = cw[:, SSD_D_INNER + G * N:].reshape(SSD_CONV, G, N)
    conv_w_g = jnp.pad(jnp.transpose(jnp.concatenate([cwx, cwb, cwc], axis=2), (1, 0, 2)),
                       ((0, 0), (0, 8 - SSD_CONV), (0, 0)))
    cbv = conv_b[0]
    conv_b_g = jnp.concatenate([cbv[:SSD_D_INNER].reshape(G, 1, W),
                                cbv[SSD_D_INNER:SSD_D_INNER + G * N].reshape(G, 1, N),
                                cbv[SSD_D_INNER + G * N:].reshape(G, 1, N)], axis=2)
    dskip_g = jnp.repeat(d_skip[0], SSD_HEAD_DIM).reshape(G, 1, W)
    normw_g = ssd_norm_w[0].reshape(G, 1, W)

    wq_t = w_query[0].T.astype(BF)
    keys = sub_keys[0].reshape(2 * PEER_HEADS, PEER_KEYS, PEER_HALF).astype(BF)
    u_b = expert_u[0].astype(BF)
    v_t = expert_v[0].T.astype(BF)

    x2 = x.reshape(T, D)
    proj, dt_raw = _in_proj(x2, mixer_norm_w, w_main, w_dt)
    dtcs, cst = _dt_prep(dt_raw.reshape(B, S, LANES), bias_row, alog_row, jnp.asarray(_tri_matrix(), BF))
    y_ssd = _ssd(proj.reshape(B, S, MAIN_W), dtcs, cst, conv_w_g, conv_b_g, dskip_g, normw_g)
    h, hnt = _merge(y_ssd.reshape(T, SSD_D_INNER), proj, x2,
                    w_ssd_branch[0].astype(BF), jnp.asarray(_band_matrices(TM_MERGE), BF),
                    w_pool_group[0].astype(BF), pool_scale, w_pool_branch[0].astype(BF),
                    w_out[0].astype(BF), ffn_norm_w, S)
    r1, vh, n, uu = _route(hnt, wq_t, keys)
    out = _experts(hnt, u_b, v_t, r1, vh, n, uu, h, final_norm_w.reshape(1, D))
    return out.reshape(B, S, D)
```
